```python
import math
import jax, jax.numpy as jnp
from jax import lax
import numpy as np

D_MODEL = 2048
BATCH = 2
SEQ = 4096
DEPTH = 4
DEC_BATCH = 128
DEC_SEQ = 1
PAST_LEN = 8192
PAGE_SIZE = 128

MLA_HEADS = 8
MLA_Q_LORA = 384
MLA_KV_LORA = 128
MLA_NOPE = 64
MLA_ROPE = 32
MLA_V = 64
MLA_SCALE = (MLA_NOPE + MLA_ROPE) ** -0.5
ROPE_THETA = 10000.0
DIFF_HEADS = 4
DIFF_KV_HEADS = 1
DIFF_HD = 64
DIFF_SCALE = DIFF_HD ** -0.5
MOBA_HEADS = 8
MOBA_KV_HEADS = 2
MOBA_HD = 64
MOBA_BLOCK = 256
MOBA_TOPK = 3
MOBA_SCALE = MOBA_HD ** -0.5
N_BUCKETS = 32
MAX_DISTANCE = 128
D_FF = 4 * D_MODEL
PLE_DIM = 256
N_BRANCH = 3
EPS = 1e-6
Q_BLOCK = 128
NEG_INF = -1e30

MLA_OUT = MLA_HEADS * MLA_V
DIFF_OUT = DIFF_HEADS * 2 * DIFF_HD
MOBA_OUT = MOBA_HEADS * MOBA_HD
SPLITS = (MLA_Q_LORA, MLA_KV_LORA, MLA_ROPE,
          DIFF_HEADS * 2 * DIFF_HD, DIFF_KV_HEADS * 2 * DIFF_HD, DIFF_KV_HEADS * 2 * DIFF_HD,
          MOBA_HEADS * MOBA_HD, MOBA_KV_HEADS * MOBA_HD, MOBA_KV_HEADS * MOBA_HD)
IN_COLS = sum(SPLITS)

kernel_name = "hybrid_mla_diff_moba_decoder_step"


def _rmsnorm(x, g):
    xf = x.astype(jnp.float32)
    y = xf * lax.rsqrt(jnp.mean(xf * xf, axis=-1, keepdims=True) + EPS)
    return (y * g.astype(jnp.float32)).astype(x.dtype)


def _split(a, sizes):
    return jnp.split(a, np.cumsum(sizes)[:-1].tolist(), axis=-1)


def _rope(x, pos):
    half = x.shape[-1] // 2
    inv = ROPE_THETA ** (-jnp.arange(half, dtype=jnp.float32) / half)
    ang = pos.astype(jnp.float32)[:, None] * inv[None, :]
    cos = jnp.cos(ang)[None, :, None, :]
    sin = jnp.sin(ang)[None, :, None, :]
    xf = x.astype(jnp.float32)
    x1, x2 = xf[..., :half], xf[..., half:]
    return jnp.concatenate([x1 * cos - x2 * sin, x2 * cos + x1 * sin], axis=-1).astype(x.dtype)


def _t5_bucket(n):
    max_exact = N_BUCKETS // 2
    nf = jnp.maximum(n, max_exact).astype(jnp.float32)
    large = max_exact + (jnp.log(nf / max_exact) * ((N_BUCKETS - max_exact) / math.log(MAX_DISTANCE / max_exact))).astype(jnp.int32)
    return jnp.where(n < max_exact, n, jnp.minimum(large, N_BUCKETS - 1))


def _rel_bias(table, q_pos, k_pos):
    b = _t5_bucket(jnp.maximum(q_pos[:, None] - k_pos[None, :], 0))
    return jnp.moveaxis(table[b], -1, 0).astype(jnp.float32)


def _sweep_queries(f, q_arrays, q_pos):
    tq = q_pos.shape[0]
    if tq <= Q_BLOCK or tq % Q_BLOCK:
        return f(*q_arrays, q_pos)
    nb = tq // Q_BLOCK
    blocked = tuple(jnp.moveaxis(a.reshape(a.shape[0], nb, Q_BLOCK, *a.shape[2:]), 1, 0) for a in q_arrays)
    out = lax.map(lambda xs: f(*xs[:-1], xs[-1]), blocked + (q_pos.reshape(nb, Q_BLOCK),))
    out = jnp.moveaxis(out, 0, 1)
    return out.reshape(out.shape[0], tq, *out.shape[3:])


def _gather_pages(pool, layer, page_table):
    g = pool[layer, page_table]
    return g.reshape(g.shape[0], g.shape[1] * g.shape[2], *g.shape[3:])


def _mla_attend(q_lat, q_rope, c_all, kr_all, q_pos):
    k_pos = jnp.arange(c_all.shape[1])

    def block(ql, qr, pos):
        s = jnp.einsum('bqhc,bkc->bhqk', ql, c_all) + jnp.einsum('bqhr,bkr->bhqk', qr, kr_all)
        s = jnp.where(k_pos[None, None, None, :] <= pos[None, None, :, None], s.astype(jnp.float32) * MLA_SCALE, NEG_INF)
        pr = jax.nn.softmax(s, axis=-1).astype(c_all.dtype)
        return jnp.einsum('bhqk,bkc->bqhc', pr, c_all)
    return _sweep_queries(block, (q_lat, q_rope), q_pos)


def _diff_attend(q1, q2, k1, k2, v, q_pos, bias_table, lam):
    B, _, H, d = q1.shape
    L, G = k1.shape[1], k1.shape[2]
    R = H // G
    k_pos = jnp.arange(L)

    def block(a1, a2, pos):
        tq = pos.shape[0]
        bias = _rel_bias(bias_table, pos, k_pos).reshape(G, R, tq, L)
        mask = k_pos[None, :] <= pos[:, None]

        def probs(q, k):
            s = jnp.einsum('bqgrd,bkgd->bgrqk', q.reshape(B, tq, G, R, d), k).astype(jnp.float32) * DIFF_SCALE + bias
            return jax.nn.softmax(jnp.where(mask, s, NEG_INF), axis=-1)
        att = (probs(a1, k1) - lam * probs(a2, k2)).astype(v.dtype)
        return jnp.einsum('bgrqk,bkge->bqgre', att, v).reshape(B, tq, H, v.shape[-1])
    return _sweep_queries(block, (q1, q2), q_pos)


def _moba_attend(q, k_all, v_all, q_pos, bias_table):
    B, _, H, hd = q.shape
    L, G = k_all.shape[1], k_all.shape[2]
    R = H // G
    nb = -(-L // MOBA_BLOCK)
    pad = ((0, 0), (0, nb * MOBA_BLOCK - L), (0, 0), (0, 0))
    kb = jnp.pad(k_all, pad).reshape(B, nb, MOBA_BLOCK, G, hd)
    vb = jnp.pad(v_all, pad).reshape(B, nb, MOBA_BLOCK, G, hd)
    k_mean = jnp.mean(kb, axis=2, dtype=jnp.float32).astype(k_all.dtype)
    n_sel = min(MOBA_TOPK, nb - 1)
    tab = bias_table.T
    off = jnp.arange(MOBA_BLOCK)
    b_i = jnp.arange(B)[:, None, None]
    g_i = (jnp.arange(H) // R)[None, :, None]
    h_i = jnp.arange(H)[None, :, None, None]

    def block(qb, pos):
        tq = pos.shape[0]
        qh = qb.transpose(0, 2, 1, 3) * MOBA_SCALE
        own = pos // MOBA_BLOCK
        k_own = kb[b_i, own[None, None, :], :, g_i]
        v_own = vb[b_i, own[None, None, :], :, g_i]
        kpos_own = own[:, None] * MOBA_BLOCK + off
        s_own = jnp.einsum('bhqd,bhqkd->bhqk', qh, k_own).astype(jnp.float32) \
            + tab[:, _t5_bucket(jnp.maximum(pos[:, None] - kpos_own, 0))].astype(jnp.float32)
        s_own = jnp.where(kpos_own <= pos[:, None], s_own, NEG_INF)
        if n_sel == 0:
            pr = jax.nn.softmax(s_own, axis=-1).astype(v_all.dtype)
            out = jnp.einsum('bhqk,bhqkd->bhqd', pr, v_own)
        else:
            blk_s = jnp.einsum('bqgrd,bngd->bgrqn', qb.reshape(B, tq, G, R, hd), k_mean).reshape(B, H, tq, nb).astype(jnp.float32)
            blk_s = jnp.where(jnp.arange(nb)[None, :] < own[:, None], blk_s, NEG_INF)
            _, top_i = lax.top_k(blk_s, n_sel)
            k_sel = kb[b_i[..., None], top_i, :, g_i[..., None]]
            v_sel = vb[b_i[..., None], top_i, :, g_i[..., None]]
            n_keys = n_sel * MOBA_BLOCK
            kpos_sel = (top_i[..., None] * MOBA_BLOCK + off).reshape(B, H, tq, n_keys)
            ok_sel = jnp.broadcast_to((top_i < own[:, None])[..., None], (B, H, tq, n_sel, MOBA_BLOCK)).reshape(B, H, tq, n_keys)
            s_sel = jnp.einsum('bhqd,bhqnkd->bhqnk', qh, k_sel).reshape(B, H, tq, n_keys).astype(jnp.float32) \
                + tab[h_i, _t5_bucket(jnp.maximum(pos[:, None] - kpos_sel, 0))].astype(jnp.float32)
            s_sel = jnp.where(ok_sel, s_sel, NEG_INF)
            pr = jax.nn.softmax(jnp.concatenate([s_sel, s_own], axis=-1), axis=-1).astype(v_all.dtype)
            out = jnp.einsum('bhqn,bhqnd->bhqd', pr[..., :n_keys], v_sel.reshape(B, H, tq, n_keys, hd)) \
                + jnp.einsum('bhqk,bhqkd->bhqd', pr[..., n_keys:], v_own)
        return out.transpose(0, 2, 1, 3)
    return _sweep_queries(block, (q,), q_pos)


def setup_inputs(seed: int = 0) -> dict:
    key = jax.random.key(seed)
    ks = jax.random.split(key, 64)
    ctr = [0]

    def nk():
        ctr[0] += 1
        return ks[ctr[0]]

    def nrm(shape, scale=1.0):
        return jax.random.normal(nk(), shape, jnp.float32) * scale

    def gain(shape):
        return 1.0 + 0.02 * jax.random.normal(nk(), shape, jnp.float32)

    n_pages = PAST_LEN // PAGE_SIZE
    n_used = DEC_BATCH * n_pages
    n_pool = n_used + n_used // 4
    page_table = jax.random.permutation(nk(), n_pool)[:n_used].reshape(DEC_BATCH, n_pages).astype(jnp.int32)
    pool = (DEPTH, n_pool, PAGE_SIZE)
    D = D_MODEL
    return {
        "x_prompt": nrm((BATCH, SEQ, D)),
        "x_sample": nrm((DEC_BATCH, DEC_SEQ, D)),
        "cache_mla_ckv": nrm(pool + (MLA_KV_LORA,)),
        "cache_mla_krope": nrm(pool + (MLA_ROPE,)),
        "cache_diff_k": nrm(pool + (DIFF_KV_HEADS, 2 * DIFF_HD)),
        "cache_diff_v": nrm(pool + (DIFF_KV_HEADS, 2 * DIFF_HD)),
        "cache_moba_k": nrm(pool + (MOBA_KV_HEADS, MOBA_HD)),
        "cache_moba_v": nrm(pool + (MOBA_KV_HEADS, MOBA_HD)),
        "page_table": page_table,
        "p_prompt": nrm((DEPTH, BATCH, SEQ, PLE_DIM)),
        "p_sample": nrm((DEPTH, DEC_BATCH, DEC_SEQ, PLE_DIM)),
        "rel_bias_table": nrm((N_BUCKETS, DIFF_HEADS + MOBA_HEADS), 0.2),
        "norm_pre_mix": gain((DEPTH, D)),
        "norm_post_mix": gain((DEPTH, D)),
        "norm_pre_mlp": gain((DEPTH, D)),
        "norm_post_mlp": gain((DEPTH, D)),
        "w_in": nrm((DEPTH, D, IN_COLS), D ** -0.5),
        "mla_q_norm": gain((DEPTH, MLA_Q_LORA)),
        "mla_kv_norm": gain((DEPTH, MLA_KV_LORA)),
        "mla_w_uq": nrm((DEPTH, MLA_Q_LORA, MLA_HEADS * (MLA_NOPE + MLA_ROPE)), MLA_Q_LORA ** -0.5),
        "mla_w_ukv": nrm((DEPTH, MLA_KV_LORA, MLA_HEADS * (MLA_NOPE + MLA_V)), MLA_KV_LORA ** -0.5),
        "diff_lambda_q1": nrm((DEPTH, DIFF_HD), 0.1),
        "diff_lambda_k1": nrm((DEPTH, DIFF_HD), 0.1),
        "diff_lambda_q2": nrm((DEPTH, DIFF_HD), 0.1),
        "diff_lambda_k2": nrm((DEPTH, DIFF_HD), 0.1),
        "diff_subln": gain((DEPTH, 2 * DIFF_HD)),
        "w_branch_mla": nrm((DEPTH, MLA_OUT, D), MLA_OUT ** -0.5),
        "w_branch_diff": nrm((DEPTH, DIFF_OUT, D), DIFF_OUT ** -0.5),
        "w_branch_moba": nrm((DEPTH, MOBA_OUT, D), MOBA_OUT ** -0.5),
        "w_gate": nrm((DEPTH, D, N_BRANCH * D), D ** -0.5),
        "b_gate": nrm((DEPTH, N_BRANCH * D), 0.02),
        "w_out": nrm((DEPTH, D, D), D ** -0.5),
        "w_up": nrm((DEPTH, D, D_FF), D ** -0.5),
        "w_down": nrm((DEPTH, D_FF, D), D_FF ** -0.5),
        "ple_norm": gain((DEPTH, D)),
        "w_ple_gate": nrm((DEPTH, D, D), D ** -0.5),
        "w_ple": nrm((DEPTH, PLE_DIM, D), PLE_DIM ** -0.5),
    }


def reference(x_prompt, x_sample, cache_mla_ckv, cache_mla_krope, cache_diff_k, cache_diff_v,
              cache_moba_k, cache_moba_v, page_table, p_prompt, p_sample, rel_bias_table,
              norm_pre_mix, norm_post_mix, norm_pre_mlp, norm_post_mlp, w_in, mla_q_norm, mla_kv_norm,
              mla_w_uq, mla_w_ukv, diff_lambda_q1, diff_lambda_k1, diff_lambda_q2, diff_lambda_k2,
              diff_subln, w_branch_mla, w_branch_diff, w_branch_moba, w_gate, b_gate, w_out,
              w_up, w_down, ple_norm, w_ple_gate, w_ple):
    caches = (cache_mla_ckv, cache_mla_krope, cache_diff_k, cache_diff_v, cache_moba_k, cache_moba_v)
    bias_diff = rel_bias_table[:, :DIFF_HEADS]
    bias_moba = rel_bias_table[:, DIFF_HEADS:]

    def layer(i, x, p, q_pos, past):
        B, T, _ = x.shape
        h = _rmsnorm(x, norm_pre_mix[i])
        cq, ckv, kr, dq, dk, dv, mq, mk, mv = _split(h @ w_in[i], SPLITS)
        cq = _rmsnorm(cq, mla_q_norm[i])
        q = (cq @ mla_w_uq[i]).reshape(B, T, MLA_HEADS, MLA_NOPE + MLA_ROPE)
        q_nope, q_rope = q[..., :MLA_NOPE], _rope(q[..., MLA_NOPE:], q_pos)
        ckv = _rmsnorm(ckv, mla_kv_norm[i])
        kr = _rope(kr[:, :, None, :], q_pos)[:, :, 0, :]
        w_ukv = mla_w_ukv[i].reshape(MLA_KV_LORA, MLA_HEADS, MLA_NOPE + MLA_V)
        w_uk, w_uv = w_ukv[..., :MLA_NOPE], w_ukv[..., MLA_NOPE:]
        q_lat = jnp.einsum('bqhd,chd->bqhc', q_nope, w_uk)
        dk = dk.reshape(B, T, DIFF_KV_HEADS, 2 * DIFF_HD)
        dv = dv.reshape(B, T, DIFF_KV_HEADS, 2 * DIFF_HD)
        mk = mk.reshape(B, T, MOBA_KV_HEADS, MOBA_HD)
        mv = mv.reshape(B, T, MOBA_KV_HEADS, MOBA_HD)
        new_rows = (ckv, kr, dk, dv, mk, mv)
        if past is None:
            full = new_rows
        else:
            full = tuple(jnp.concatenate([pa, nw], axis=1) for pa, nw in zip(past, new_rows))
        c_all, kr_all, dk_all, dv_all, mk_all, mv_all = full
        o_lat = _mla_attend(q_lat, q_rope, c_all, kr_all, q_pos)
        o_mla = jnp.einsum('bqhc,chd->bqhd', o_lat, w_uv).reshape(B, T, MLA_OUT)
        lam_init = 0.8 - 0.6 * math.exp(-0.3 * i)
        lam = (jnp.exp(jnp.sum(diff_lambda_q1[i] * diff_lambda_k1[i], dtype=jnp.float32))
               - jnp.exp(jnp.sum(diff_lambda_q2[i] * diff_lambda_k2[i], dtype=jnp.float32)) + lam_init)
        dq = dq.reshape(B, T, DIFF_HEADS, 2, DIFF_HD)
        o = _diff_attend(dq[..., 0, :], dq[..., 1, :], dk_all[..., :DIFF_HD], dk_all[..., DIFF_HD:],
                         dv_all, q_pos, bias_diff, lam)
        o_diff = (_rmsnorm(o, diff_subln[i]) * (1.0 - lam_init)).reshape(B, T, DIFF_OUT)
        o_moba = _moba_attend(mq.reshape(B, T, MOBA_HEADS, MOBA_HD), mk_all, mv_all, q_pos, bias_moba).reshape(B, T, MOBA_OUT)
        gates = jax.nn.sigmoid((h @ w_gate[i] + b_gate[i]).astype(jnp.float32)).astype(x.dtype).reshape(B, T, N_BRANCH, D_MODEL)
        merged = (gates[:, :, 0] * (o_mla @ w_branch_mla[i])
                  + gates[:, :, 1] * (o_diff @ w_branch_diff[i])
                  + gates[:, :, 2] * (o_moba @ w_branch_moba[i]))
        x = x + _rmsnorm(merged @ w_out[i], norm_post_mix[i])
        f = jnp.square(jax.nn.relu(_rmsnorm(x, norm_pre_mlp[i]) @ w_up[i])) @ w_down[i]
        x = x + _rmsnorm(f, norm_post_mlp[i])
        g = jax.nn.sigmoid((_rmsnorm(x, ple_norm[i]) @ w_ple_gate[i]).astype(jnp.float32)).astype(x.dtype)
        x = x + g * (p[i] @ w_ple[i])
        return x, new_rows

    def run(x, p, q_pos, use_cache):
        rows = []
        for i in range(DEPTH):
            past = tuple(_gather_pages(c, i, page_table) for c in caches) if use_cache else None
            x, r = layer(i, x, p, q_pos, past)
            rows.append(r)
        return (x,) + tuple(jnp.stack(col, axis=0) for col in zip(*rows))

    past_len = page_table.shape[1] * cache_mla_ckv.shape[2]
    pos_prompt = jnp.arange(x_prompt.shape[1], dtype=jnp.int32)
    pos_sample = past_len + jnp.arange(x_sample.shape[1], dtype=jnp.int32)
    (y_prompt, new_mla_ckv_prompt, new_mla_krope_prompt, new_diff_k_prompt, new_diff_v_prompt,
     new_moba_k_prompt, new_moba_v_prompt) = run(x_prompt, p_prompt, pos_prompt, False)
    (y_sample, new_mla_ckv_sample, new_mla_krope_sample, new_diff_k_sample, new_diff_v_sample,
     new_moba_k_sample, new_moba_v_sample) = run(x_sample, p_sample, pos_sample, True)
    return (y_prompt, y_sample,
            new_mla_ckv_prompt, new_mla_krope_prompt, new_diff_k_prompt, new_diff_v_prompt,
            new_moba_k_prompt, new_moba_v_prompt,
            new_mla_ckv_sample, new_mla_krope_sample, new_diff_k_sample, new_diff_v_sample,
            new_moba_k_sample, new_moba_v_sample)
```

```python
import functools
import math

import jax
import jax.numpy as jnp
from jax import lax
from jax.experimental import pallas as pl
from jax.experimental.pallas import tpu as pltpu

MLA_HEADS = 8
MLA_Q_LORA = 384
MLA_KV_LORA = 128
MLA_NOPE = 64
MLA_ROPE = 32
MLA_V = 64
MLA_SCALE = (MLA_NOPE + MLA_ROPE) ** -0.5
ROPE_THETA = 10000.0
DIFF_HEADS = 4
DIFF_HD = 64
DIFF_SCALE = DIFF_HD ** -0.5
MOBA_HEADS = 8
MOBA_KV_HEADS = 2
MOBA_HD = 64
MOBA_BLOCK = 256
MOBA_TOPK = 3
MOBA_SCALE = MOBA_HD ** -0.5
N_BUCKETS = 32
MAX_DISTANCE = 128
N_BRANCH = 3
EPS = 1e-6
NEG_INF = -1e30

LANES = 128
ATT_BLOCK = 256
PAGES_PER_CHUNK = 16
VMEM_LIMIT = 48 * 1024 * 1024

F32 = jnp.float32
BF16 = jnp.bfloat16

_C_CQ = 0
_C_CKV = _C_CQ + MLA_Q_LORA
_C_KR = _C_CKV + LANES
_C_KRR = _C_KR + LANES
_C_DQ = _C_KRR + LANES
_C_DK = _C_DQ + DIFF_HEADS * LANES
_C_DV = _C_DK + LANES
_C_MQ = _C_DV + LANES
_C_MK = _C_MQ + MOBA_HEADS * MOBA_HD
_C_MV = _C_MK + LANES
_C_END = _C_MV + LANES


def _rms(x, g):
    return x * lax.rsqrt(jnp.mean(x * x, axis=-1, keepdims=True) + EPS) * g


def _dot(a, b):
    return jnp.dot(a, b, preferred_element_type=F32)


def _dot_nt(a, b):
    return lax.dot_general(a, b, (((1,), (1,)), ((), ())), preferred_element_type=F32)


def _lane_iota(shape):
    return lax.broadcasted_iota(jnp.int32, shape, len(shape) - 1)


def _params(sem):
    return pltpu.CompilerParams(dimension_semantics=sem, vmem_limit_bytes=VMEM_LIMIT)


def _tile(n, want):
    t = min(n, want)
    assert n % t == 0, (n, t)
    return t


def _t5_bucket(n):
    max_exact = N_BUCKETS // 2
    nf = jnp.maximum(n, max_exact).astype(F32)
    large = max_exact + (jnp.log(nf / max_exact)
                         * ((N_BUCKETS - max_exact) / math.log(MAX_DISTANCE / max_exact))).astype(jnp.int32)
    return jnp.where(n < max_exact, n, jnp.minimum(large, N_BUCKETS - 1))


def _bias_gather_kernel(table_ref, bucket_ref, out_ref):
    h = pl.program_id(0)
    b = bucket_ref[...]
    acc = jnp.zeros(b.shape, F32)
    for n in range(N_BUCKETS):
        acc = jnp.where(b == n, table_ref[n, h], acc)
    out_ref[0] = acc


def _bias_gather(table, buckets):
    n_heads = table.shape[1]
    r, c = buckets.shape
    return pl.pallas_call(
        _bias_gather_kernel,
        grid=(n_heads,),
        in_specs=[pl.BlockSpec(memory_space=pltpu.SMEM),
                  pl.BlockSpec((r, c), lambda h: (0, 0))],
        out_specs=pl.BlockSpec((1, r, c), lambda h: (h, 0, 0)),
        out_shape=jax.ShapeDtypeStruct((n_heads, r, c), F32),
        compiler_params=_params(("arbitrary",)),
        name="bias_gather",
    )(table, buckets)


def _pre_kernel(x_ref, g_ref, win_ref, qn_ref, kvn_ref, wq_ref, wuk_ref, cos_ref, sin_ref,
                ckv_o, kr_o, dk_o, dv_o, mk_o, mv_o,
                kcat_o, qcat_o, dq_o, dkb_o, dvb_o, mq_o, mkdup_o, mvb_o, kmean_o, *, with_kmean):
    h = _rms(x_ref[...], g_ref[...]).astype(BF16)
    proj = _dot(h, win_ref[...])
    cos = cos_ref[...]
    sin = sin_ref[...]
    ckv_n = _rms(proj[:, _C_CKV:_C_KR], kvn_ref[...])
    kr = proj[:, _C_KR:_C_KRR] * cos + proj[:, _C_KRR:_C_DQ] * sin
    dk = proj[:, _C_DK:_C_DV]
    dv = proj[:, _C_DV:_C_MQ]
    mk = proj[:, _C_MK:_C_MV]
    mv = proj[:, _C_MV:_C_END]
    ckv_o[...] = ckv_n
    kr_o[...] = kr[:, :MLA_ROPE]
    dk_o[...] = dk
    dv_o[...] = dv
    mk_o[...] = mk
    mv_o[...] = mv
    kcat_o[:, :LANES] = ckv_n.astype(BF16)
    kcat_o[:, LANES:] = kr.astype(BF16)
    dq_o[...] = (proj[:, _C_DQ:_C_DK] * DIFF_SCALE).astype(BF16)
    dkb_o[...] = dk.astype(BF16)
    dvb_o[...] = dv.astype(BF16)
    mq_o[...] = (proj[:, _C_MQ:_C_MK] * MOBA_SCALE).astype(BF16)
    mvb_o[...] = mv.astype(BF16)
    low = _lane_iota(mk.shape) < MOBA_HD
    swapped = pltpu.roll(mk, MOBA_HD, 1)
    mkdup_o[:, :LANES] = jnp.where(low, mk, swapped).astype(BF16)
    mkdup_o[:, LANES:] = jnp.where(low, swapped, mk).astype(BF16)
    if with_kmean:
        nblk = mk.shape[0] // MOBA_BLOCK
        km = jnp.sum(mk.reshape(nblk, MOBA_BLOCK, LANES), axis=1) * (1.0 / MOBA_BLOCK)
        kmean_o[...] = jnp.broadcast_to(km[:, None, :], kmean_o.shape)
    else:
        kmean_o[...] = jnp.zeros(kmean_o.shape, F32)
    cq = _rms(proj[:, _C_CQ:_C_CKV], qn_ref[...]).astype(BF16)
    q_all = _dot(cq, wq_ref[...])
    nq = MLA_HEADS * LANES
    for hd in range(MLA_HEADS):
        q_nope = q_all[:, hd * LANES:(hd + 1) * LANES].astype(BF16)
        q_lat = _dot(q_nope, wuk_ref[hd])
        q_rope = (q_all[:, nq + hd * LANES:nq + (hd + 1) * LANES] * cos
                  + q_all[:, 2 * nq + hd * LANES:2 * nq + (hd + 1) * LANES] * sin)
        qcat_o[:, 2 * hd * LANES:(2 * hd + 1) * LANES] = q_lat.astype(BF16)
        qcat_o[:, (2 * hd + 1) * LANES:(2 * hd + 2) * LANES] = q_rope.astype(BF16)


def _pre_call(x, g, win, qn, kvn, wq, wuk, cos, sin, *, with_kmean):
    n, d = x.shape
    tm = _tile(n, 256)
    nkm = max(tm // MOBA_BLOCK, 1)
    row = lambda w: pl.BlockSpec((tm, w), lambda i: (i, 0))
    full = lambda a: pl.BlockSpec(a.shape, lambda i: (0,) * a.ndim)
    out_shapes = [
        jax.ShapeDtypeStruct((n, LANES), F32), jax.ShapeDtypeStruct((n, MLA_ROPE), F32),
        jax.ShapeDtypeStruct((n, LANES), F32), jax.ShapeDtypeStruct((n, LANES), F32),
        jax.ShapeDtypeStruct((n, LANES), F32), jax.ShapeDtypeStruct((n, LANES), F32),
        jax.ShapeDtypeStruct((n, 2 * LANES), BF16), jax.ShapeDtypeStruct((n, 2 * MLA_HEADS * LANES), BF16),
        jax.ShapeDtypeStruct((n, DIFF_HEADS * LANES), BF16),
        jax.ShapeDtypeStruct((n, LANES), BF16), jax.ShapeDtypeStruct((n, LANES), BF16),
        jax.ShapeDtypeStruct((n, MOBA_HEADS * MOBA_HD), BF16), jax.ShapeDtypeStruct((n, 2 * LANES), BF16),
        jax.ShapeDtypeStruct((n, LANES), BF16),
        jax.ShapeDtypeStruct((n // tm * nkm, 8, LANES), F32),
    ]
    out_specs = [row(LANES), row(MLA_ROPE), row(LANES), row(LANES), row(LANES), row(LANES),
                 row(2 * LANES), row(2 * MLA_HEADS * LANES), row(DIFF_HEADS * LANES), row(LANES), row(LANES),
                 row(MOBA_HEADS * MOBA_HD), row(2 * LANES), row(LANES),
                 pl.BlockSpec((nkm, 8, LANES), lambda i: (i, 0, 0))]
    return pl.pallas_call(
        functools.partial(_pre_kernel, with_kmean=with_kmean),
        grid=(n // tm,),
        in_specs=[row(d), full(g), full(win), full(qn), full(kvn), full(wq), full(wuk), row(LANES), row(LANES)],
        out_specs=out_specs,
        out_shape=out_shapes,
        compiler_params=_params(("parallel",)),
        name="pre_mix",
    )(x, g, win, qn, kvn, wq, wuk, cos, sin)


def _softmax_step(s, v, m_scr, l_scr, acc_scr, idx):
    m_prev = m_scr[idx]
    m_next = jnp.maximum(m_prev, jnp.max(s, axis=1, keepdims=True))
    p = jnp.exp(s - m_next[:, :1])
    alpha = jnp.exp(m_prev - m_next)
    l_scr[idx] = alpha * l_scr[idx] + jnp.sum(p, axis=1, keepdims=True)
    m_scr[idx] = m_next
    acc_scr[idx] = acc_scr[idx] * alpha + _dot(p.astype(BF16), v)


def _init_state(m_scr, l_scr, acc_scr):
    m_scr[...] = jnp.full(m_scr.shape, NEG_INF, F32)
    l_scr[...] = jnp.zeros(l_scr.shape, F32)
    acc_scr[...] = jnp.zeros(acc_scr.shape, F32)


def _causal_mask(t):
    return lax.broadcasted_iota(jnp.int32, (t, t), 0) >= lax.broadcasted_iota(jnp.int32, (t, t), 1)


def _mla_out_proj(o_heads, wuv_ref, o_ref):
    for j in range(MLA_HEADS // 2):
        pair = jnp.concatenate([o_heads[2 * j], o_heads[2 * j + 1]], axis=1).astype(BF16)
        o_ref[:, j * LANES:(j + 1) * LANES] = _dot(pair, wuv_ref[j]).astype(o_ref.dtype)


def _mla_attn_kernel(q_ref, k_ref, wuv_ref, o_ref, m_scr, l_scr, acc_scr):
    qi = pl.program_id(1)
    ki = pl.program_id(2)

    @pl.when(ki == 0)
    def _():
        _init_state(m_scr, l_scr, acc_scr)

    def step(diagonal):
        k = k_ref[...]
        v = k[:, :LANES]
        for hd in range(MLA_HEADS):
            s = _dot_nt(q_ref[:, 2 * hd * LANES:(2 * hd + 2) * LANES], k) * MLA_SCALE
            if diagonal:
                s = jnp.where(_causal_mask(s.shape[0]), s, NEG_INF)
            _softmax_step(s, v, m_scr, l_scr, acc_scr, hd)

    pl.when(ki < qi)(lambda: step(False))

    @pl.when(ki == qi)
    def _():
        step(True)
        _mla_out_proj([acc_scr[hd] / l_scr[hd] for hd in range(MLA_HEADS)], wuv_ref, o_ref)


def _attn_grid_specs(bsz, t, widths_q, widths_k):
    tb = ATT_BLOCK
    nq = t // tb
    q_specs = [pl.BlockSpec((tb, w), lambda b, qi, ki: (b * nq + qi, 0)) for w in widths_q]
    k_specs = [pl.BlockSpec((tb, w), lambda b, qi, ki: (b * nq + jnp.minimum(ki, qi), 0)) for w in widths_k]
    return (bsz, nq, nq), q_specs, k_specs


def _mla_attn_call(qcat, kcat, wuv, bsz, t):
    n = qcat.shape[0]
    grid, q_specs, k_specs = _attn_grid_specs(bsz, t, [qcat.shape[1]], [kcat.shape[1]])
    nq = grid[1]
    tb = ATT_BLOCK
    return pl.pallas_call(
        _mla_attn_kernel,
        grid=grid,
        in_specs=q_specs + k_specs + [pl.BlockSpec(wuv.shape, lambda b, qi, ki: (0, 0, 0))],
        out_specs=pl.BlockSpec((tb, MLA_HEADS * MLA_V), lambda b, qi, ki: (b * nq + qi, 0)),
        out_shape=jax.ShapeDtypeStruct((n, MLA_HEADS * MLA_V), BF16),
        scratch_shapes=[pltpu.VMEM((MLA_HEADS, tb, LANES), F32)] * 3,
        compiler_params=_params(("parallel", "parallel", "arbitrary")),
        name="mla_attn",
    )(qcat, kcat, wuv)


def _diff_lambda(lq1_ref, lk1_ref, lq2_ref, lk2_ref, lam_init):
    return (jnp.exp(jnp.sum(lq1_ref[...] * lk1_ref[...], axis=1, keepdims=True))
            - jnp.exp(jnp.sum(lq2_ref[...] * lk2_ref[...], axis=1, keepdims=True)) + lam_init)


def _diff_finish(n1, n2, lam, subln, lam_init):
    return _rms(n1 - lam * n2, subln) * (1.0 - lam_init)


def _diff_attn_kernel(bfar_ref, q_ref, k_ref, v_ref, btile_ref, lq1_ref, lk1_ref, lq2_ref, lk2_ref, subln_ref,
                      o_ref, m_scr, l_scr, acc_scr, *, lam_init):
    qi = pl.program_id(1)
    ki = pl.program_id(2)

    @pl.when(ki == 0)
    def _():
        _init_state(m_scr, l_scr, acc_scr)

    def step(rel):
        k = k_ref[...]
        v = v_ref[...]
        for hd in range(DIFF_HEADS):
            qb = q_ref[:, hd * LANES:(hd + 1) * LANES]
            low = _lane_iota(qb.shape) < DIFF_HD
            zero = jnp.zeros_like(qb)
            bias = bfar_ref[hd] if rel is None else btile_ref[hd, rel]
            for half, q in enumerate((jnp.where(low, qb, zero), jnp.where(low, zero, qb))):
                s = _dot_nt(q, k) + bias
                if rel == 0:
                    s = jnp.where(_causal_mask(s.shape[0]), s, NEG_INF)
                _softmax_step(s, v, m_scr, l_scr, acc_scr, half * DIFF_HEADS + hd)

    pl.when(ki < qi - 1)(lambda: step(None))
    pl.when(ki == qi - 1)(lambda: step(1))

    @pl.when(ki == qi)
    def _():
        step(0)
        lam = _diff_lambda(lq1_ref, lk1_ref, lq2_ref, lk2_ref, lam_init)
        for hd in range(DIFF_HEADS):
            n1 = acc_scr[hd] / l_scr[hd]
            n2 = acc_scr[DIFF_HEADS + hd] / l_scr[DIFF_HEADS + hd]
            o_ref[:, hd * LANES:(hd + 1) * LANES] = _diff_finish(
                n1, n2, lam, subln_ref[...], lam_init).astype(o_ref.dtype)


def _diff_attn_call(dq, dkb, dvb, btile, bfar, lams, subln, bsz, t, lam_init):
    n = dq.shape[0]
    grid, q_specs, k_specs = _attn_grid_specs(bsz, t, [dq.shape[1]], [LANES, LANES])
    nq = grid[1]
    tb = ATT_BLOCK
    const = lambda a: pl.BlockSpec(a.shape, lambda b, qi, ki: (0,) * a.ndim)
    return pl.pallas_call(
        functools.partial(_diff_attn_kernel, lam_init=lam_init),
        grid=grid,
        in_specs=[pl.BlockSpec(memory_space=pltpu.SMEM)] + q_specs + k_specs + [const(btile)]
        + [const(a) for a in lams] + [const(subln)],
        out_specs=pl.BlockSpec((tb, DIFF_HEADS * LANES), lambda b, qi, ki: (b * nq + qi, 0)),
        out_shape=jax.ShapeDtypeStruct((n, DIFF_HEADS * LANES), BF16),
        scratch_shapes=[pltpu.VMEM((2 * DIFF_HEADS, tb, LANES), F32)] * 3,
        compiler_params=_params(("parallel", "parallel", "arbitrary")),
        name="diff_attn",
    )(bfar, dq, dkb, dvb, btile, *lams, subln)


def _topk_select(scores, n_valid, n_blocks, n_sel):
    lane = _lane_iota(scores.shape)
    sm = jnp.where(lane < n_valid, scores, NEG_INF)
    rank = jnp.zeros(scores.shape, F32)
    for j in range(n_blocks):
        col = sm[:, j:j + 1]
        rank = rank + jnp.where(col > sm, 1.0, jnp.where(col == sm, jnp.where(lane > j, 1.0, 0.0), 0.0))
    return jnp.where(rank < n_sel, jnp.where(lane < n_valid, 1.0, 0.0), 0.0)


def _moba_query(q_ref, hd):
    qb = q_ref[:, (hd // 2) * LANES:(hd // 2 + 1) * LANES]
    low = _lane_iota(qb.shape) < MOBA_HD
    zero = jnp.zeros_like(qb)
    return jnp.where(low, qb, zero) if hd % 2 == 0 else jnp.where(low, zero, qb)


def _moba_attn_kernel(bfar_ref, q_ref, k_ref, v_ref, kmean_ref, btile_ref, o_ref,
                      m_scr, l_scr, acc_scr, sel_scr, *, n_blocks, n_sel):
    qi = pl.program_id(1)
    ki = pl.program_id(2)
    rep = MOBA_HEADS // MOBA_KV_HEADS

    @pl.when(ki == 0)
    def _():
        _init_state(m_scr, l_scr, acc_scr)
        km = kmean_ref[0]
        low = _lane_iota(km.shape) < MOBA_HD
        swapped = pltpu.roll(km, MOBA_HD, 1)
        kdup = (jnp.where(low, km, swapped).astype(BF16), jnp.where(low, swapped, km).astype(BF16))
        for hd in range(MOBA_HEADS):
            blk_s = _dot_nt(_moba_query(q_ref, hd), kdup[hd // rep])
            sel_scr[hd] = _topk_select(blk_s, qi, n_blocks, n_sel)

    def step(rel):
        v = v_ref[...]
        for hd in range(MOBA_HEADS):
            g = hd // rep
            bias = bfar_ref[hd] if rel is None else btile_ref[hd, rel]
            s = _dot_nt(_moba_query(q_ref, hd), k_ref[:, g * LANES:(g + 1) * LANES]) + bias
            if rel == 0:
                s = jnp.where(_causal_mask(s.shape[0]), s, NEG_INF)
            else:
                sel = sel_scr[hd]
                chosen = jnp.max(jnp.where(_lane_iota(sel.shape) == ki, sel, 0.0), axis=1, keepdims=True)
                s = jnp.where(chosen > 0.0, s, NEG_INF)
            _softmax_step(s, v, m_scr, l_scr, acc_scr, hd)

    pl.when(ki < qi - 1)(lambda: step(None))
    pl.when(ki == qi - 1)(lambda: step(1))

    @pl.when(ki == qi)
    def _():
        step(0)
        for j in range(rep):
            a = acc_scr[j] / l_scr[j]
            b = acc_scr[j + rep] / l_scr[j + rep]
            low = _lane_iota(a.shape) < MOBA_HD
            o_ref[:, j * LANES:(j + 1) * LANES] = jnp.where(low, a, b).astype(o_ref.dtype)


def _moba_attn_call(mq, mkdup, mvb, kmean_pad, btile, bfar, bsz, t):
    n = mq.shape[0]
    grid, q_specs, k_specs = _attn_grid_specs(bsz, t, [mq.shape[1]], [2 * LANES, LANES])
    nq = grid[1]
    tb = ATT_BLOCK
    n_blocks = t // MOBA_BLOCK
    n_sel = min(MOBA_TOPK, n_blocks - 1)
    return pl.pallas_call(
        functools.partial(_moba_attn_kernel, n_blocks=n_blocks, n_sel=n_sel),
        grid=grid,
        in_specs=[pl.BlockSpec(memory_space=pltpu.SMEM)] + q_specs + k_specs
        + [pl.BlockSpec((1, LANES, LANES), lambda b, qi, ki: (b, 0, 0)),
           pl.BlockSpec(btile.shape, lambda b, qi, ki: (0, 0, 0, 0))],
        out_specs=pl.BlockSpec((tb, MOBA_HEADS * MOBA_HD), lambda b, qi, ki: (b * nq + qi, 0)),
        out_shape=jax.ShapeDtypeStruct((n, MOBA_HEADS * MOBA_HD), BF16),
        scratch_shapes=[pltpu.VMEM((MOBA_HEADS, tb, LANES), F32)] * 4,
        compiler_params=_params(("parallel", "parallel", "arbitrary")),
        name="moba_attn",
    )(bfar, mq, mkdup, mvb, kmean_pad, btile)


def _round_bf16(x):
    return x.astype(BF16).astype(F32)


def _sample_attn_kernel(pt_ref, qm_ref, qd_ref, qo_ref, nkm_ref, ndk_ref, ndv_ref, nmk_ref, nmv_ref,
                        bias_d_ref, bias_o_ref, bnew_d_ref, bnew_o_ref,
                        ckv_hbm, kr_hbm, dk_hbm, dv_hbm, mk_hbm, mv_hbm,
                        omla_ref, odiff_ref, omoba_ref,
                        ckv_buf, kr_buf, dk_buf, dv_buf, mk_buf, mv_buf, sems,
                        *, layer, n_samples, n_chunks, n_sel):
    page = ckv_buf.shape[1] // PAGES_PER_CHUNK
    chunk = ckv_buf.shape[1]
    blocks_per_chunk = chunk // MOBA_BLOCK
    n_blocks = n_chunks * blocks_per_chunk
    pools = (ckv_hbm, kr_hbm, dk_hbm, dv_hbm, mk_hbm, mv_hbm)
    bufs = (ckv_buf, kr_buf, dk_buf, dv_buf, mk_buf, mv_buf)

    def page_copy(which, b, c, p, slot):
        pid = pt_ref[b, c * PAGES_PER_CHUNK + p]
        return pltpu.make_async_copy(pools[which].at[layer, pid],
                                     bufs[which].at[slot, pl.ds(p * page, page)],
                                     sems.at[slot, which])

    def start_chunk(b, c, slot):
        for p in range(PAGES_PER_CHUNK):
            for which in range(len(pools)):
                page_copy(which, b, c, p, slot).start()

    def wait_chunk(b, c, slot):
        for p in range(PAGES_PER_CHUNK):
            for which in range(len(pools)):
                page_copy(which, b, c, p, slot).wait()

    def online(s, v, state):
        m, l, acc = state
        m_new = jnp.maximum(m, jnp.max(s, axis=1, keepdims=True))
        p = jnp.exp(s - m_new)
        alpha = jnp.exp(m - m_new)
        return m_new, alpha * l + jnp.sum(p, axis=1, keepdims=True), alpha * acc + _dot(p.astype(BF16), v)

    def finish(state, s_new, v_new):
        m, l, acc = state
        m_new = jnp.maximum(m, s_new)
        p_new = jnp.exp(s_new - m_new)
        alpha = jnp.exp(m - m_new)
        return (alpha * acc + _round_bf16(p_new) * _round_bf16(v_new)) / (alpha * l + p_new)

    def new_score(q, k_new):
        return jnp.sum(_round_bf16(q) * _round_bf16(k_new), axis=1, keepdims=True)

    start_chunk(0, 0, 0)

    def sample_body(b, carry):
        qm = qm_ref[b].astype(BF16)
        qd = qd_ref[b].astype(BF16)
        qo = qo_ref[b].astype(BF16)
        fresh = (jnp.full((8, 1), NEG_INF, F32), jnp.zeros((8, 1), F32), jnp.zeros((8, LANES), F32))
        st_m, st_d = fresh, fresh
        blk_m, blk_l, blk_a, blk_mean = [], [], [], []
        for c in range(n_chunks):
            slot = c % 2
            if c + 1 < n_chunks:
                start_chunk(b, c + 1, 1 - slot)
            else:
                @pl.when(b + 1 < n_samples)
                def _():
                    start_chunk(b + 1, 0, 1 - slot)
            wait_chunk(b, c, slot)
            ckv = ckv_buf[slot].astype(BF16)
            kr = kr_buf[slot].astype(BF16)
            s = (_dot_nt(qm[:, :LANES], ckv) + _dot_nt(qm[:, LANES:LANES + MLA_ROPE], kr)) * MLA_SCALE
            st_m = online(s, ckv, st_m)
            s = _dot_nt(qd, dk_buf[slot].astype(BF16)) + bias_d_ref[:, c * chunk:(c + 1) * chunk]
            st_d = online(s, dv_buf[slot].astype(BF16), st_d)
            for j in range(blocks_per_chunk):
                lo = j * MOBA_BLOCK
                kb = mk_buf[slot, lo:lo + MOBA_BLOCK, :]
                blk_mean.append(jnp.sum(kb, axis=0, keepdims=True) * (1.0 / MOBA_BLOCK))
                s = _dot_nt(qo, kb.astype(BF16)) + bias_o_ref[:, c * chunk + lo:c * chunk + lo + MOBA_BLOCK]
                m_b = jnp.max(s, axis=1, keepdims=True)
                p = jnp.exp(s - m_b)
                blk_m.append(m_b)
                blk_l.append(jnp.sum(p, axis=1, keepdims=True))
                blk_a.append(_dot(p.astype(BF16), mv_buf[slot, lo:lo + MOBA_BLOCK, :].astype(BF16)))
        nkm = nkm_ref[b]
        omla_ref[b] = finish(st_m, new_score(qm, nkm) * MLA_SCALE, nkm[:, :LANES])
        odiff_ref[b] = finish(st_d, new_score(qd, ndk_ref[b]) + bnew_d_ref[:, :1], ndv_ref[b])
        kmean = jnp.concatenate(blk_mean, axis=0).astype(BF16)
        sel = _topk_select(_dot_nt(qo, kmean), n_blocks, n_blocks, n_sel)
        s_own = new_score(qo, nmk_ref[b]) + bnew_o_ref[:, :1]
        m_all = s_own
        for j in range(n_blocks):
            m_all = jnp.maximum(m_all, jnp.where(sel[:, j:j + 1] > 0.0, blk_m[j], NEG_INF))
        p_own = jnp.exp(s_own - m_all)
        l_all = p_own
        acc = _round_bf16(p_own) * _round_bf16(nmv_ref[b])
        for j in range(n_blocks):
            w = jnp.where(sel[:, j:j + 1] > 0.0, jnp.exp(blk_m[j] - m_all), 0.0)
            l_all = l_all + w * blk_l[j]
            acc = acc + w * blk_a[j]
        omoba_ref[b] = acc / l_all
        return carry

    lax.fori_loop(0, n_samples, sample_body, 0)


def _sample_attn_call(page_table, qm, qd, qo, nkm, ndk, ndv, nmk, nmv, bias_d, bias_o, bnew_d, bnew_o,
                      caches, layer):
    n_samples, n_pages = page_table.shape
    page = caches[0].shape[2]
    assert n_pages % PAGES_PER_CHUNK == 0 and (n_pages // PAGES_PER_CHUNK) % 2 == 0
    n_chunks = n_pages // PAGES_PER_CHUNK
    chunk = PAGES_PER_CHUNK * page
    assert chunk % MOBA_BLOCK == 0
    n_blocks = n_pages * page // MOBA_BLOCK
    n_sel = min(MOBA_TOPK, n_blocks)
    vmem = pl.BlockSpec(memory_space=pltpu.VMEM)
    out = jax.ShapeDtypeStruct((n_samples, 8, LANES), F32)
    return pl.pallas_call(
        functools.partial(_sample_attn_kernel, layer=layer, n_samples=n_samples, n_chunks=n_chunks, n_sel=n_sel),
        in_specs=[pl.BlockSpec(memory_space=pltpu.SMEM)] + [vmem] * 12 + [pl.BlockSpec(memory_space=pl.ANY)] * 6,
        out_specs=[vmem] * 3,
        out_shape=[out, out, out],
        scratch_shapes=[pltpu.VMEM((2, chunk, c.shape[3]), F32) for c in caches]
        + [pltpu.SemaphoreType.DMA((2, len(caches)))],
        compiler_params=pltpu.CompilerParams(vmem_limit_bytes=VMEM_LIMIT),
        name="sample_attn",
    )(page_table, qm, qd, qo, nkm, ndk, ndv, nmk, nmv, bias_d, bias_o, bnew_d, bnew_o, *caches)


def _sample_post_kernel(omla_ref, odiff_ref, omoba_ref, wuv_ref, lq1_ref, lk1_ref, lq2_ref, lk2_ref, subln_ref,
                        mla_o, diff_o, moba_o, *, lam_init):
    _mla_out_proj([omla_ref[:, hd * LANES:(hd + 1) * LANES] for hd in range(MLA_HEADS)], wuv_ref, mla_o)
    lam = _diff_lambda(lq1_ref, lk1_ref, lq2_ref, lk2_ref, lam_init)
    for hd in range(DIFF_HEADS):
        n1 = odiff_ref[:, hd * LANES:(hd + 1) * LANES]
        n2 = odiff_ref[:, (DIFF_HEADS + hd) * LANES:(DIFF_HEADS + hd + 1) * LANES]
        diff_o[:, hd * LANES:(hd + 1) * LANES] = _diff_finish(
            n1, n2, lam, subln_ref[...], lam_init).astype(diff_o.dtype)
    rep = MOBA_HEADS // MOBA_KV_HEADS
    for j in range(rep):
        a = omoba_ref[:, j * LANES:(j + 1) * LANES]
        b = omoba_ref[:, (j + rep) * LANES:(j + rep + 1) * LANES]
        low = _lane_iota(a.shape) < MOBA_HD
        moba_o[:, j * LANES:(j + 1) * LANES] = jnp.where(low, a, b).astype(moba_o.dtype)


def _sample_post_call(omla, odiff, omoba, wuv, lams, subln, lam_init):
    n = omla.shape[0]
    out = jax.ShapeDtypeStruct((n, 4 * LANES), BF16)
    return pl.pallas_call(
        functools.partial(_sample_post_kernel, lam_init=lam_init),
        out_shape=[out, out, out],
        compiler_params=pltpu.CompilerParams(vmem_limit_bytes=VMEM_LIMIT),
        name="sample_post",
    )(omla.reshape(n, -1), odiff.reshape(n, -1), omoba.reshape(n, -1), wuv, *lams, subln)


def _merge_kernel(x_ref, g_ref, omla_ref, odiff_ref, omoba_ref, wg0_ref, wg1_ref, wg2_ref,
                  bg0_ref, bg1_ref, bg2_ref, wb0_ref, wb1_ref, wb2_ref, o_ref, h_scr):
    @pl.when(pl.program_id(1) == 0)
    def _():
        h_scr[...] = _rms(x_ref[...], g_ref[...]).astype(BF16)

    h = h_scr[...]
    merged = None
    for o_in, wg, bg, wb in ((omla_ref, wg0_ref, bg0_ref, wb0_ref), (odiff_ref, wg1_ref, bg1_ref, wb1_ref),
                             (omoba_ref, wg2_ref, bg2_ref, wb2_ref)):
        gate = jax.nn.sigmoid(_dot(h, wg[...]) + bg[...])
        term = gate * _dot(o_in[...], wb[...])
        merged = term if merged is None else merged + term
    o_ref[...] = merged.astype(o_ref.dtype)


def _merge_call(x, g, omla, odiff, omoba, wgate, bgate, wb_mla, wb_diff, wb_moba):
    n, d = x.shape
    tm = _tile(n, 512)
    tn = _tile(d, 512)
    nd = d // tn
    row = lambda w: pl.BlockSpec((tm, w), lambda i, j: (i, 0))
    gate_w = [pl.BlockSpec((d, tn), functools.partial(lambda i, j, br: (0, br * nd + j), br=br)) for br in range(3)]
    gate_b = [pl.BlockSpec((1, tn), functools.partial(lambda i, j, br: (0, br * nd + j), br=br)) for br in range(3)]
    branch_w = [pl.BlockSpec((w.shape[0], tn), lambda i, j: (0, j)) for w in (wb_mla, wb_diff, wb_moba)]
    return pl.pallas_call(
        _merge_kernel,
        grid=(n // tm, nd),
        in_specs=[row(d), pl.BlockSpec(g.shape, lambda i, j: (0, 0)),
                  row(omla.shape[1]), row(odiff.shape[1]), row(omoba.shape[1])] + gate_w + gate_b + branch_w,
        out_specs=pl.BlockSpec((tm, tn), lambda i, j: (i, j)),
        out_shape=jax.ShapeDtypeStruct((n, d), BF16),
        scratch_shapes=[pltpu.VMEM((tm, d), BF16)],
        compiler_params=_params(("parallel", "arbitrary")),
        name="gated_merge",
    )(x, g, omla, odiff, omoba, wgate, wgate, wgate, bgate, bgate, bgate, wb_mla, wb_diff, wb_moba)


def _out_proj_kernel(x_ref, merged_ref, wout_ref, gpost_ref, gmlp_ref, x1_ref, hn_ref):
    x1 = x_ref[...] + _rms(_dot(merged_ref[...], wout_ref[...]), gpost_ref[...])
    x1_ref[...] = x1
    hn_ref[...] = _rms(x1, gmlp_ref[...]).astype(hn_ref.dtype)


def _out_proj_call(x, merged, wout, gpost, gmlp):
    n, d = x.shape
    tm = _tile(n, 256)
    row = pl.BlockSpec((tm, d), lambda i: (i, 0))
    full = lambda a: pl.BlockSpec(a.shape, lambda i: (0, 0))
    return pl.pallas_call(
        _out_proj_kernel,
        grid=(n // tm,),
        in_specs=[row, row, full(wout), full(gpost), full(gmlp)],
        out_specs=[row, row],
        out_shape=[jax.ShapeDtypeStruct((n, d), F32), jax.ShapeDtypeStruct((n, d), BF16)],
        compiler_params=_params(("parallel",)),
        name="out_proj",
    )(x, merged, wout, gpost, gmlp)


def _mlp_kernel(hn_ref, wup_ref, wdown_ref, x1_ref, gpost_ref, x2_ref, acc_scr):
    k = pl.program_id(1)

    @pl.when(k == 0)
    def _():
        acc_scr[...] = jnp.zeros(acc_scr.shape, F32)

    u = jnp.square(jnp.maximum(_dot(hn_ref[...], wup_ref[...]), 0.0)).astype(BF16)
    acc_scr[...] += _dot(u, wdown_ref[...])

    @pl.when(k == pl.num_programs(1) - 1)
    def _():
        x2_ref[...] = x1_ref[...] + _rms(acc_scr[...], gpost_ref[...])


def _mlp_call(hn, wup, wdown, x1, gpost):
    n, d = x1.shape
    dff = wup.shape[1]
    tm = _tile(n, 512)
    tf = _tile(dff, 1024)
    row = pl.BlockSpec((tm, d), lambda i, k: (i, 0))
    return pl.pallas_call(
        _mlp_kernel,
        grid=(n // tm, dff // tf),
        in_specs=[row, pl.BlockSpec((d, tf), lambda i, k: (0, k)), pl.BlockSpec((tf, d), lambda i, k: (k, 0)),
                  row, pl.BlockSpec(gpost.shape, lambda i, k: (0, 0))],
        out_specs=row,
        out_shape=jax.ShapeDtypeStruct((n, d), F32),
        scratch_shapes=[pltpu.VMEM((tm, d), F32)],
        compiler_params=_params(("parallel", "arbitrary")),
        name="relu2_mlp",
    )(hn, wup, wdown, x1, gpost)


def _ple_kernel(x2_ref, p_ref, gple_ref, wpg_ref, wple_ref, x3_ref):
    x2 = x2_ref[...]
    gate = jax.nn.sigmoid(_dot(_rms(x2, gple_ref[...]).astype(BF16), wpg_ref[...]))
    x3_ref[...] = x2 + gate * _dot(p_ref[...].astype(BF16), wple_ref[...])


def _ple_call(x2, p, gple, wpg, wple):
    n, d = x2.shape
    tm = _tile(n, 256)
    row = lambda w: pl.BlockSpec((tm, w), lambda i: (i, 0))
    full = lambda a: pl.BlockSpec(a.shape, lambda i: (0, 0))
    return pl.pallas_call(
        _ple_kernel,
        grid=(n // tm,),
        in_specs=[row(d), row(p.shape[1]), full(gple), full(wpg), full(wple)],
        out_specs=row(d),
        out_shape=jax.ShapeDtypeStruct((n, d), F32),
        compiler_params=_params(("parallel",)),
        name="ple_gate",
    )(x2, p, gple, wpg, wple)


def _pad_cols(w, width):
    return jnp.pad(w, ((0, 0), (0, width - w.shape[1])))


def _rot_cols(w):
    half = w.shape[1] // 2
    return jnp.concatenate([-w[:, half:], w[:, :half]], axis=1)


def _prep_layer(w_in, w_uq, w_ukv, wb_moba):
    sizes = (MLA_Q_LORA, MLA_KV_LORA, MLA_ROPE, DIFF_HEADS * 2 * DIFF_HD, 2 * DIFF_HD, 2 * DIFF_HD,
             MOBA_HEADS * MOBA_HD, MOBA_KV_HEADS * MOBA_HD, MOBA_KV_HEADS * MOBA_HD)
    offs = [0]
    for s in sizes:
        offs.append(offs[-1] + s)
    cq, ckv, kr, dq, dk, dv, mq, mk, mv = [w_in[:, offs[i]:offs[i + 1]] for i in range(len(sizes))]
    win = jnp.concatenate([cq, ckv, _pad_cols(kr, LANES), _pad_cols(_rot_cols(kr), LANES),
                           dq, dk, dv, mq, mk, mv], axis=1).astype(BF16)
    assert win.shape[1] == _C_END
    per_head = MLA_NOPE + MLA_ROPE
    nope, rope, rot = [], [], []
    for hd in range(MLA_HEADS):
        blk = w_uq[:, hd * per_head:(hd + 1) * per_head]
        nope.append(_pad_cols(blk[:, :MLA_NOPE], LANES))
        rope.append(_pad_cols(blk[:, MLA_NOPE:], LANES))
        rot.append(_pad_cols(_rot_cols(blk[:, MLA_NOPE:]), LANES))
    wq = jnp.concatenate(nope + rope + rot, axis=1).astype(BF16)
    w3 = w_ukv.reshape(MLA_KV_LORA, MLA_HEADS, MLA_NOPE + MLA_V)
    wuk = jnp.pad(jnp.transpose(w3[..., :MLA_NOPE], (1, 2, 0)), ((0, 0), (0, LANES - MLA_NOPE), (0, 0))).astype(BF16)
    wuv_h = jnp.transpose(w3[..., MLA_NOPE:], (1, 0, 2))
    zero = jnp.zeros_like(wuv_h[0])
    wuv = jnp.stack([jnp.concatenate([jnp.concatenate([wuv_h[2 * j], zero], axis=1),
                                      jnp.concatenate([zero, wuv_h[2 * j + 1]], axis=1)], axis=0)
                     for j in range(MLA_HEADS // 2)]).astype(BF16)
    rep = MOBA_HEADS // MOBA_KV_HEADS
    order = [hd for j in range(rep) for hd in (j, j + rep)]
    wbm = jnp.concatenate([wb_moba[hd * MOBA_HD:(hd + 1) * MOBA_HD] for hd in order], axis=0).astype(BF16)
    return win, wq, wuk, wuv, wbm


def _rope_tables(pos):
    half = MLA_ROPE // 2
    inv = ROPE_THETA ** (-jnp.arange(half, dtype=F32) / half)
    ang = pos.astype(F32)[:, None] * inv[None, :]
    cos = jnp.cos(ang)
    sin = jnp.sin(ang)
    return (_pad_cols(jnp.concatenate([cos, cos], axis=1), LANES),
            _pad_cols(jnp.concatenate([sin, sin], axis=1), LANES))


def kernel(x_prompt, x_sample, cache_mla_ckv, cache_mla_krope, cache_diff_k, cache_diff_v, cache_moba_k,
           cache_moba_v, page_table, p_prompt, p_sample, rel_bias_table, norm_pre_mix, norm_post_mix,
           norm_pre_mlp, norm_post_mlp, w_in, mla_q_norm, mla_kv_norm, mla_w_uq, mla_w_ukv, diff_lambda_q1,
           diff_lambda_k1, diff_lambda_q2, diff_lambda_k2, diff_subln, w_branch_mla, w_branch_diff,
           w_branch_moba, w_gate, b_gate, w_out, w_up, w_down, ple_norm, w_ple_gate, w_ple):
    depth = w_in.shape[0]
    bsz, t, d = x_prompt.shape
    n_s = x_sample.shape[0]
    assert x_sample.shape[1] == 1 and t % ATT_BLOCK == 0
    n_pool, page = cache_mla_ckv.shape[1], cache_mla_ckv.shape[2]
    past = page_table.shape[1] * page
    caches = (cache_mla_ckv, cache_mla_krope,
              cache_diff_k.reshape(depth, n_pool, page, LANES), cache_diff_v.reshape(depth, n_pool, page, LANES),
              cache_moba_k.reshape(depth, n_pool, page, LANES), cache_moba_v.reshape(depth, n_pool, page, LANES))

    cos_p, sin_p = _rope_tables(jnp.arange(t, dtype=jnp.int32))
    cos_p, sin_p = jnp.tile(cos_p, (bsz, 1)), jnp.tile(sin_p, (bsz, 1))
    cos_s, sin_s = _rope_tables(jnp.full((n_s,), past, dtype=jnp.int32))
    tb = ATT_BLOCK
    ii = jnp.arange(tb, dtype=jnp.int32)
    dist = jnp.concatenate([jnp.maximum(ii[:, None] - ii[None, :], 0), ii[:, None] - ii[None, :] + tb], axis=0)
    btile = _bias_gather(rel_bias_table, _t5_bucket(dist)).reshape(-1, 2, tb, tb)
    far_bucket = N_BUCKETS - 1
    assert 2 * tb - tb + 1 >= MAX_DISTANCE
    bfar = rel_bias_table[far_bucket]
    dist_s = (past - jnp.arange(past, dtype=jnp.int32))[None, :]
    bias_keys = _bias_gather(rel_bias_table, _t5_bucket(dist_s))[:, 0, :]
    bias_d = jnp.concatenate([bias_keys[:DIFF_HEADS], bias_keys[:DIFF_HEADS]], axis=0)
    bias_o = bias_keys[DIFF_HEADS:]
    bnew = jnp.broadcast_to(rel_bias_table[0][:, None], (DIFF_HEADS + MOBA_HEADS, LANES))
    bnew_d = jnp.concatenate([bnew[:DIFF_HEADS], bnew[:DIFF_HEADS]], axis=0)
    bnew_o = bnew[DIFF_HEADS:]
    btile_d, btile_o = btile[:DIFF_HEADS], btile[DIFF_HEADS:]
    bfar_d, bfar_o = bfar[:DIFF_HEADS], bfar[DIFF_HEADS:]

    xp = x_prompt.reshape(bsz * t, d)
    xs = x_sample.reshape(n_s, d)
    rows_p, rows_s = [], []
    r2 = lambda a: a.reshape(1, -1)
    lane = jnp.arange(LANES)
    for i in range(depth):
        lam_init = 0.8 - 0.6 * math.exp(-0.3 * i)
        win, wq, wuk, wuv, wbm = _prep_layer(w_in[i], mla_w_uq[i], mla_w_ukv[i], w_branch_moba[i])
        lams = [r2(diff_lambda_q1[i]), r2(diff_lambda_k1[i]), r2(diff_lambda_q2[i]), r2(diff_lambda_k2[i])]
        subln = r2(diff_subln[i])
        wgate, bgate = w_gate[i].astype(BF16), r2(b_gate[i])
        wb_mla, wb_diff = w_branch_mla[i].astype(BF16), w_branch_diff[i].astype(BF16)
        wout, wup, wdown = w_out[i].astype(BF16), w_up[i].astype(BF16), w_down[i].astype(BF16)
        wpg, wple = w_ple_gate[i].astype(BF16), w_ple[i].astype(BF16)
        pre_w = (r2(norm_pre_mix[i]), win, r2(mla_q_norm[i]), r2(mla_kv_norm[i]), wq, wuk)

        def tail(x, omla, odiff, omoba, p):
            merged = _merge_call(x, r2(norm_pre_mix[i]), omla, odiff, omoba, wgate, bgate, wb_mla, wb_diff, wbm)
            x1, hn = _out_proj_call(x, merged, wout, r2(norm_post_mix[i]), r2(norm_pre_mlp[i]))
            x2 = _mlp_call(hn, wup, wdown, x1, r2(norm_post_mlp[i]))
            return _ple_call(x2, p, r2(ple_norm[i]), wpg, wple)

        (ckv, kr, dk, dv, mk, mv, kcat, qcat, dq, dkb, dvb, mq, mkdup, mvb, kmean) = _pre_call(
            xp, *pre_w, cos_p, sin_p, with_kmean=True)
        rows_p.append((ckv, kr, dk, dv, mk, mv))
        nblk = t // MOBA_BLOCK
        kmean_pad = jnp.pad(kmean[:, 0, :].reshape(bsz, nblk, LANES), ((0, 0), (0, LANES - nblk), (0, 0)))
        omla = _mla_attn_call(qcat, kcat, wuv, bsz, t)
        odiff = _diff_attn_call(dq, dkb, dvb, btile_d, bfar_d, lams, subln, bsz, t, lam_init)
        omoba = _moba_attn_call(mq, mkdup, mvb, kmean_pad, btile_o, bfar_o, bsz, t)
        xp = tail(xp, omla, odiff, omoba, p_prompt[i].reshape(bsz * t, -1))

        (ckv, kr, dk, dv, mk, mv, kcat, qcat, dq, dkb, dvb, mq, mkdup, mvb, _) = _pre_call(
            xs, *pre_w, cos_s, sin_s, with_kmean=False)
        rows_s.append((ckv, kr, dk, dv, mk, mv))
        qm = qcat.astype(F32).reshape(n_s, MLA_HEADS, 2 * LANES)
        dq3 = dq.astype(F32).reshape(n_s, DIFF_HEADS, LANES)
        qd = jnp.concatenate([jnp.where(lane < DIFF_HD, dq3, 0.0), jnp.where(lane < DIFF_HD, 0.0, dq3)], axis=1)
        mq3 = mq.astype(F32).reshape(n_s, MOBA_HEADS, MOBA_HD)
        rep = MOBA_HEADS // MOBA_KV_HEADS
        qo = jnp.concatenate([jnp.pad(mq3[:, :rep], ((0, 0), (0, 0), (0, MOBA_HD))),
                              jnp.pad(mq3[:, rep:], ((0, 0), (0, 0), (MOBA_HD, 0)))], axis=1)
        nkm = jnp.concatenate([ckv, _pad_cols(kr, LANES)], axis=1)[:, None, :]
        a_mla, a_diff, a_moba = _sample_attn_call(
            page_table, qm, qd, qo, nkm, dk[:, None, :], dv[:, None, :], mk[:, None, :], mv[:, None, :],
            bias_d, bias_o, bnew_d, bnew_o, caches, i)
        omla, odiff, omoba = _sample_post_call(a_mla, a_diff, a_moba, wuv, lams, subln, lam_init)
        xs = tail(xs, omla, odiff, omoba, p_sample[i].reshape(n_s, -1))

    def stack(rows, lead):
        cols = list(zip(*rows))
        shapes = [(MLA_KV_LORA,), (MLA_ROPE,), (1, 2 * DIFF_HD), (1, 2 * DIFF_HD),
                  (MOBA_KV_HEADS, MOBA_HD), (MOBA_KV_HEADS, MOBA_HD)]
        return tuple(jnp.stack(c, axis=0).reshape((depth,) + lead + s) for c, s in zip(cols, shapes))

    return ((xp.reshape(bsz, t, d), xs.reshape(n_s, 1, d)) + stack(rows_p, (bsz, t)) + stack(rows_s, (n_s, 1)))
```

```python
import functools
import math

import jax
import jax.numpy as jnp
from jax import lax
from jax.experimental import pallas as pl
from jax.experimental.pallas import tpu as pltpu

MLA_HEADS = 8
MLA_Q_LORA = 384
MLA_KV_LORA = 128
MLA_NOPE = 64
MLA_ROPE = 32
MLA_V = 64
MLA_SCALE = (MLA_NOPE + MLA_ROPE) ** -0.5
ROPE_THETA = 10000.0
DIFF_HEADS = 4
DIFF_HD = 64
DIFF_SCALE = DIFF_HD ** -0.5
MOBA_HEADS = 8
MOBA_KV_HEADS = 2
MOBA_HD = 64
MOBA_BLOCK = 256
MOBA_TOPK = 3
MOBA_SCALE = MOBA_HD ** -0.5
N_BUCKETS = 32
MAX_DISTANCE = 128
N_BRANCH = 3
EPS = 1e-6
NEG_INF = -1e30

LANES = 128
SUBLANES = 8
ATT_BLOCK = 256
ATT_ROWS = 8
PAGES_PER_CHUNK = 16
VMEM_LIMIT = 48 * 1024 * 1024

F32 = jnp.float32
BF16 = jnp.bfloat16

_C_CQ = 0
_C_CKV = _C_CQ + MLA_Q_LORA
_C_KR = _C_CKV + LANES
_C_KRR = _C_KR + LANES
_C_DQ = _C_KRR + LANES
_C_DK = _C_DQ + DIFF_HEADS * LANES
_C_DV = _C_DK + LANES
_C_MQ = _C_DV + LANES
_C_MK = _C_MQ + MOBA_HEADS * MOBA_HD
_C_MV = _C_MK + LANES
_C_END = _C_MV + LANES


def _rms(x, g):
    return x * lax.rsqrt(jnp.mean(x * x, axis=-1, keepdims=True) + EPS) * g


def _dot(a, b):
    return jnp.dot(a, b, preferred_element_type=F32)


def _dot_nt(a, b):
    return lax.dot_general(a, b, (((1,), (1,)), ((), ())), preferred_element_type=F32)


def _lane_iota(shape):
    return lax.broadcasted_iota(jnp.int32, shape, len(shape) - 1)


def _row_iota(shape):
    return lax.broadcasted_iota(jnp.int32, shape, 0)


def _params(sem):
    return pltpu.CompilerParams(dimension_semantics=sem, vmem_limit_bytes=VMEM_LIMIT)


def _tile(n, want):
    t = min(n, want)
    assert n % t == 0, (n, t)
    return t


def _t5_bucket(n):
    max_exact = N_BUCKETS // 2
    nf = jnp.maximum(n, max_exact).astype(F32)
    large = max_exact + (jnp.log(nf / max_exact)
                         * ((N_BUCKETS - max_exact) / math.log(MAX_DISTANCE / max_exact))).astype(jnp.int32)
    return jnp.where(n < max_exact, n, jnp.minimum(large, N_BUCKETS - 1))


def _bias_gather_kernel(table_ref, bucket_ref, out_ref):
    h = pl.program_id(0)
    b = bucket_ref[...]
    acc = jnp.zeros(b.shape, F32)
    for n in range(N_BUCKETS):
        acc = jnp.where(b == n, table_ref[n, h], acc)
    out_ref[0] = acc


def _bias_gather(table, buckets):
    n_heads = table.shape[1]
    r, c = buckets.shape
    return pl.pallas_call(
        _bias_gather_kernel,
        grid=(n_heads,),
        in_specs=[pl.BlockSpec(memory_space=pltpu.SMEM),
                  pl.BlockSpec((r, c), lambda h: (0, 0))],
        out_specs=pl.BlockSpec((1, r, c), lambda h: (h, 0, 0)),
        out_shape=jax.ShapeDtypeStruct((n_heads, r, c), F32),
        compiler_params=_params(("arbitrary",)),
        name="bias_gather",
    )(table, buckets)


def _pre_kernel(x_ref, g_ref, win_ref, qn_ref, kvn_ref, wq_ref, wuk_ref, cos_ref, sin_ref,
                ckv_o, kr_o, dk_o, dv_o, mk_o, mv_o,
                kcat_o, ckvt_o, dkb_o, dvt_o, mkb_o, mvt_o, qm_o, qd_o, qo_o, kmean_o, *, with_kmean):
    h = _rms(x_ref[...], g_ref[...]).astype(BF16)
    proj = _dot(h, win_ref[...])
    cos = cos_ref[...]
    sin = sin_ref[...]
    ckv_n = _rms(proj[:, _C_CKV:_C_KR], kvn_ref[...])
    kr = proj[:, _C_KR:_C_KRR] * cos + proj[:, _C_KRR:_C_DQ] * sin
    dk = proj[:, _C_DK:_C_DV]
    dv = proj[:, _C_DV:_C_MQ]
    mk = proj[:, _C_MK:_C_MV]
    mv = proj[:, _C_MV:_C_END]
    ckv_o[...] = ckv_n
    kr_o[...] = kr[:, :MLA_ROPE]
    dk_o[...] = dk
    dv_o[...] = dv
    mk_o[...] = mk
    mv_o[...] = mv
    kcat_o[:, :LANES] = ckv_n.astype(BF16)
    kcat_o[:, LANES:] = kr.astype(BF16)
    ckvt_o[...] = ckv_n.T.astype(BF16)
    dkb_o[...] = dk.astype(BF16)
    dvt_o[...] = dv.T.astype(BF16)
    mkb_o[...] = mk.astype(BF16)
    mvt_o[...] = mv.T.astype(BF16)
    low = _lane_iota(dk.shape) < DIFF_HD
    for hd in range(DIFF_HEADS):
        qb = proj[:, _C_DQ + hd * LANES:_C_DQ + (hd + 1) * LANES] * DIFF_SCALE
        qd_o[0, hd] = jnp.where(low, qb, 0.0).astype(BF16)
        qd_o[0, DIFF_HEADS + hd] = jnp.where(low, 0.0, qb).astype(BF16)
    rep = MOBA_HEADS // MOBA_KV_HEADS
    for j in range(MOBA_HEADS // 2):
        qb = proj[:, _C_MQ + j * LANES:_C_MQ + (j + 1) * LANES] * MOBA_SCALE
        swapped = pltpu.roll(qb, MOBA_HD, 1)
        for half in range(2):
            hd = 2 * j + half
            src = qb if hd // rep == half else swapped
            qo_o[0, hd] = (jnp.where(low, src, 0.0) if hd // rep == 0 else jnp.where(low, 0.0, src)).astype(BF16)
    if with_kmean:
        nblk = mk.shape[0] // MOBA_BLOCK
        km = jnp.sum(mk.reshape(nblk, MOBA_BLOCK, LANES), axis=1) * (1.0 / MOBA_BLOCK)
        kmean_o[...] = jnp.broadcast_to(km[:, None, :], kmean_o.shape)
    else:
        kmean_o[...] = jnp.zeros(kmean_o.shape, F32)
    cq = _rms(proj[:, _C_CQ:_C_CKV], qn_ref[...]).astype(BF16)
    q_all = _dot(cq, wq_ref[...])
    nq = MLA_HEADS * LANES
    for hd in range(MLA_HEADS):
        q_nope = q_all[:, hd * LANES:(hd + 1) * LANES].astype(BF16)
        q_lat = _dot(q_nope, wuk_ref[hd])
        q_rope = (q_all[:, nq + hd * LANES:nq + (hd + 1) * LANES] * cos
                  + q_all[:, 2 * nq + hd * LANES:2 * nq + (hd + 1) * LANES] * sin)
        qm_o[0, hd, :, :LANES] = q_lat.astype(BF16)
        qm_o[0, hd, :, LANES:] = q_rope.astype(BF16)


def _pre_call(x, g, win, qn, kvn, wq, wuk, cos, sin, *, with_kmean):
    n, d = x.shape
    tm = _tile(n, ATT_BLOCK)
    nt = n // tm
    nkm = max(tm // MOBA_BLOCK, 1)
    row = lambda w: pl.BlockSpec((tm, w), lambda i: (i, 0))
    col = pl.BlockSpec((LANES, tm), lambda i: (0, i))
    grp = lambda w: pl.BlockSpec((1, ATT_ROWS, tm, w), lambda i: (i, 0, 0, 0))
    full = lambda a: pl.BlockSpec(a.shape, lambda i: (0,) * a.ndim)
    sds = jax.ShapeDtypeStruct
    outs = [
        (sds((n, LANES), F32), row(LANES)), (sds((n, MLA_ROPE), F32), row(MLA_ROPE)),
        (sds((n, LANES), F32), row(LANES)), (sds((n, LANES), F32), row(LANES)),
        (sds((n, LANES), F32), row(LANES)), (sds((n, LANES), F32), row(LANES)),
        (sds((n, 2 * LANES), BF16), row(2 * LANES)), (sds((LANES, n), BF16), col),
        (sds((n, LANES), BF16), row(LANES)), (sds((LANES, n), BF16), col),
        (sds((n, LANES), BF16), row(LANES)), (sds((LANES, n), BF16), col),
        (sds((nt, ATT_ROWS, tm, 2 * LANES), BF16), grp(2 * LANES)),
        (sds((nt, ATT_ROWS, tm, LANES), BF16), grp(LANES)),
        (sds((nt, ATT_ROWS, tm, LANES), BF16), grp(LANES)),
        (sds((nt * nkm, SUBLANES, LANES), F32), pl.BlockSpec((nkm, SUBLANES, LANES), lambda i: (i, 0, 0))),
    ]
    return pl.pallas_call(
        functools.partial(_pre_kernel, with_kmean=with_kmean),
        grid=(nt,),
        in_specs=[row(d), full(g), full(win), full(qn), full(kvn), full(wq), full(wuk), row(LANES), row(LANES)],
        out_specs=[o[1] for o in outs],
        out_shape=[o[0] for o in outs],
        compiler_params=_params(("parallel",)),
        name="pre_mix",
    )(x, g, win, qn, kvn, wq, wuk, cos, sin)


def _init_state(m_scr, l_scr, acc_scr):
    m_scr[...] = jnp.full(m_scr.shape, NEG_INF, F32)
    l_scr[...] = jnp.zeros(l_scr.shape, F32)
    acc_scr[...] = jnp.zeros(acc_scr.shape, F32)


def _flash_step(q_ref, k, vt, m_scr, l_scr, acc_scr, adjust):
    nq = q_ref.shape[0]
    halves = ((0, nq // 2), (nq // 2, nq))
    scores = [_dot_nt(k, q_ref[lo:hi, :]) for lo, hi in halves]
    for (lo, hi), s in zip(halves, scores):
        s = adjust(s, lo, hi)
        m_prev = m_scr[:, lo:hi]
        m_next = jnp.maximum(m_prev, jnp.max(s, axis=0, keepdims=True))
        p = jnp.exp(s - m_next)
        alpha = jnp.exp(m_prev - m_next)
        l_scr[:, lo:hi] = alpha * l_scr[:, lo:hi] + jnp.sum(p, axis=0, keepdims=True)
        m_scr[:, lo:hi] = m_next
        acc_scr[:, lo:hi] = acc_scr[:, lo:hi] * alpha + _dot(vt, p.astype(BF16))


def _causal(s, tq):
    return jnp.where(_row_iota(s.shape) <= (_lane_iota(s.shape) & (tq - 1)), s, NEG_INF)


def _row_out(acc_scr, l_scr, r, tq):
    return (acc_scr[:, r * tq:(r + 1) * tq] / l_scr[:, r * tq:(r + 1) * tq]).T


def _mla_out_proj(o_heads, wuv_ref, o_ref):
    for j in range(MLA_HEADS // 2):
        pair = jnp.concatenate([o_heads[2 * j], o_heads[2 * j + 1]], axis=1).astype(BF16)
        o_ref[:, j * LANES:(j + 1) * LANES] = _dot(pair, wuv_ref[j]).astype(o_ref.dtype)


def _mla_attn_kernel(q_ref, k_ref, vt_ref, wuv_ref, o_ref, m_scr, l_scr, acc_scr, *, tq):
    qi = pl.program_id(1)
    ki = pl.program_id(2)

    @pl.when(ki == 0)
    def _():
        _init_state(m_scr, l_scr, acc_scr)

    def step(diagonal):
        def adjust(s, lo, hi):
            s = s * MLA_SCALE
            return _causal(s, tq) if diagonal else s
        _flash_step(q_ref, k_ref[...], vt_ref[...], m_scr, l_scr, acc_scr, adjust)

    pl.when(ki < qi)(lambda: step(False))

    @pl.when(ki == qi)
    def _():
        step(True)
        _mla_out_proj([_row_out(acc_scr, l_scr, hd, tq) for hd in range(MLA_HEADS)], wuv_ref, o_ref)


def _attn_specs(bsz, t, q, k, vt):
    tb = ATT_BLOCK
    nq = t // tb
    rows = ATT_ROWS * tb
    specs = [pl.BlockSpec((rows, q.shape[1]), lambda b, qi, ki: (b * nq + qi, 0)),
             pl.BlockSpec((tb, k.shape[1]), lambda b, qi, ki: (b * nq + jnp.minimum(ki, qi), 0)),
             pl.BlockSpec((LANES, tb), lambda b, qi, ki: (0, b * nq + jnp.minimum(ki, qi)))]
    scratch = [pltpu.VMEM((1, rows), F32), pltpu.VMEM((1, rows), F32), pltpu.VMEM((LANES, rows), F32)]
    out_spec = pl.BlockSpec((tb, 4 * LANES), lambda b, qi, ki: (b * nq + qi, 0))
    return (bsz, nq, nq), specs, scratch, out_spec


def _const_spec(a):
    return pl.BlockSpec(a.shape, lambda b, qi, ki: (0,) * a.ndim)


def _mla_attn_call(q, kcat, ckvt, wuv, bsz, t):
    grid, specs, scratch, out_spec = _attn_specs(bsz, t, q, kcat, ckvt)
    return pl.pallas_call(
        functools.partial(_mla_attn_kernel, tq=ATT_BLOCK),
        grid=grid,
        in_specs=specs + [_const_spec(wuv)],
        out_specs=out_spec,
        out_shape=jax.ShapeDtypeStruct((bsz * t, 4 * LANES), BF16),
        scratch_shapes=scratch,
        compiler_params=_params(("parallel", "parallel", "arbitrary")),
        name="mla_attn",
    )(q, kcat, ckvt, wuv)


def _diff_lambda(lq1_ref, lk1_ref, lq2_ref, lk2_ref, lam_init):
    return (jnp.exp(jnp.sum(lq1_ref[...] * lk1_ref[...], axis=1, keepdims=True))
            - jnp.exp(jnp.sum(lq2_ref[...] * lk2_ref[...], axis=1, keepdims=True)) + lam_init)


def _diff_finish(n1, n2, lam, subln, lam_init):
    return _rms(n1 - lam * n2, subln) * (1.0 - lam_init)


def _biased_step(q_ref, k_ref, vt_ref, btile_ref, bfar_ref, m_scr, l_scr, acc_scr, rel, tq, chosen=None):
    def adjust(s, lo, hi):
        s = s + (bfar_ref[:, lo:hi] if rel is None else btile_ref[rel, :, lo:hi])
        if rel == 0:
            return _causal(s, tq)
        if chosen is not None:
            return jnp.where(chosen[:, lo:hi] > 0.0, s, NEG_INF)
        return s
    _flash_step(q_ref, k_ref[...], vt_ref[...], m_scr, l_scr, acc_scr, adjust)


def _diff_attn_kernel(q_ref, k_ref, vt_ref, btile_ref, bfar_ref, lq1_ref, lk1_ref, lq2_ref, lk2_ref, subln_ref,
                      o_ref, m_scr, l_scr, acc_scr, *, lam_init, tq):
    qi = pl.program_id(1)
    ki = pl.program_id(2)

    @pl.when(ki == 0)
    def _():
        _init_state(m_scr, l_scr, acc_scr)

    step = functools.partial(_biased_step, q_ref, k_ref, vt_ref, btile_ref, bfar_ref, m_scr, l_scr, acc_scr, tq=tq)
    pl.when(ki < qi - 1)(lambda: step(rel=None))
    pl.when(ki == qi - 1)(lambda: step(rel=1))

    @pl.when(ki == qi)
    def _():
        step(rel=0)
        lam = _diff_lambda(lq1_ref, lk1_ref, lq2_ref, lk2_ref, lam_init)
        for hd in range(DIFF_HEADS):
            o_ref[:, hd * LANES:(hd + 1) * LANES] = _diff_finish(
                _row_out(acc_scr, l_scr, hd, tq), _row_out(acc_scr, l_scr, DIFF_HEADS + hd, tq),
                lam, subln_ref[...], lam_init).astype(o_ref.dtype)


def _diff_attn_call(q, dkb, dvt, btile, bfar, lams, subln, bsz, t, lam_init):
    grid, specs, scratch, out_spec = _attn_specs(bsz, t, q, dkb, dvt)
    return pl.pallas_call(
        functools.partial(_diff_attn_kernel, lam_init=lam_init, tq=ATT_BLOCK),
        grid=grid,
        in_specs=specs + [_const_spec(a) for a in (btile, bfar, *lams, subln)],
        out_specs=out_spec,
        out_shape=jax.ShapeDtypeStruct((bsz * t, 4 * LANES), BF16),
        scratch_shapes=scratch,
        compiler_params=_params(("parallel", "parallel", "arbitrary")),
        name="diff_attn",
    )(q, dkb, dvt, btile, bfar, *lams, subln)


def _topk_rows(scores, n_valid, n_blocks, n_sel):
    row = _row_iota(scores.shape)
    sm = jnp.where(row < n_valid, scores, NEG_INF)
    rank = jnp.zeros(scores.shape, F32)
    for j in range(n_blocks):
        r = sm[j:j + 1, :]
        rank = rank + jnp.where(r > sm, 1.0, jnp.where(r == sm, jnp.where(row > j, 1.0, 0.0), 0.0))
    return jnp.where(rank < n_sel, jnp.where(row < n_valid, 1.0, 0.0), 0.0)


def _topk_lanes(scores, n_blocks, n_sel):
    lane = _lane_iota(scores.shape)
    sm = jnp.where(lane < n_blocks, scores, NEG_INF)
    rank = jnp.zeros(scores.shape, F32)
    for j in range(n_blocks):
        c = sm[:, j:j + 1]
        rank = rank + jnp.where(c > sm, 1.0, jnp.where(c == sm, jnp.where(lane > j, 1.0, 0.0), 0.0))
    return jnp.where(rank < n_sel, jnp.where(lane < n_blocks, 1.0, 0.0), 0.0)


def _moba_attn_kernel(q_ref, k_ref, vt_ref, kmean_ref, btile_ref, bfar_ref, o_ref,
                      m_scr, l_scr, acc_scr, sel_scr, *, n_blocks, n_sel, tq):
    qi = pl.program_id(1)
    ki = pl.program_id(2)
    rep = MOBA_HEADS // MOBA_KV_HEADS

    @pl.when(ki == 0)
    def _():
        _init_state(m_scr, l_scr, acc_scr)
        blk_s = _dot_nt(kmean_ref[0].astype(BF16), q_ref[...])
        sel_scr[...] = _topk_rows(blk_s, qi, n_blocks, n_sel)

    def step(rel):
        sel = sel_scr[...]
        chosen = jnp.max(jnp.where(_row_iota(sel.shape) == ki, sel, 0.0), axis=0, keepdims=True)
        _biased_step(q_ref, k_ref, vt_ref, btile_ref, bfar_ref, m_scr, l_scr, acc_scr, rel, tq, chosen)

    pl.when(ki < qi - 1)(lambda: step(None))
    pl.when(ki == qi - 1)(lambda: step(1))

    @pl.when(ki == qi)
    def _():
        step(0)
        for j in range(rep):
            a = _row_out(acc_scr, l_scr, j, tq)
            b = _row_out(acc_scr, l_scr, j + rep, tq)
            low = _lane_iota(a.shape) < MOBA_HD
            o_ref[:, j * LANES:(j + 1) * LANES] = jnp.where(low, a, b).astype(o_ref.dtype)


def _moba_attn_call(q, mkb, mvt, kmean_pad, btile, bfar, bsz, t):
    grid, specs, scratch, out_spec = _attn_specs(bsz, t, q, mkb, mvt)
    n_blocks = t // MOBA_BLOCK
    n_sel = min(MOBA_TOPK, n_blocks - 1)
    nbp = kmean_pad.shape[1]
    return pl.pallas_call(
        functools.partial(_moba_attn_kernel, n_blocks=n_blocks, n_sel=n_sel, tq=ATT_BLOCK),
        grid=grid,
        in_specs=specs + [pl.BlockSpec((1, nbp, LANES), lambda b, qi, ki: (b, 0, 0)),
                          _const_spec(btile), _const_spec(bfar)],
        out_specs=out_spec,
        out_shape=jax.ShapeDtypeStruct((bsz * t, 4 * LANES), BF16),
        scratch_shapes=scratch + [pltpu.VMEM((nbp, ATT_ROWS * ATT_BLOCK), F32)],
        compiler_params=_params(("parallel", "parallel", "arbitrary")),
        name="moba_attn",
    )(q, mkb, mvt, kmean_pad, btile, bfar)


def _round_bf16(x):
    return x.astype(BF16).astype(F32)


def _sample_attn_kernel(pt_ref, qm_ref, qd_ref, qo_ref, nkm_ref, ndk_ref, ndv_ref, nmk_ref, nmv_ref,
                        bias_d_ref, bias_o_ref, bnew_d_ref, bnew_o_ref,
                        ckv_hbm, krt_hbm, dk_hbm, dv_hbm, mkt_hbm, mvt_hbm,
                        omla_ref, odiff_ref, omoba_ref,
                        ckv_buf, krt_buf, dk_buf, dv_buf, mkt_buf, mvt_buf, sems,
                        *, layer, n_samples, n_chunks, n_sel):
    chunk = ckv_buf.shape[1]
    page = chunk // PAGES_PER_CHUNK
    blocks_per_chunk = chunk // MOBA_BLOCK
    n_blocks = n_chunks * blocks_per_chunk
    pools = (ckv_hbm, krt_hbm, dk_hbm, dv_hbm, mkt_hbm, mvt_hbm)
    bufs = (ckv_buf, krt_buf, dk_buf, dv_buf, mkt_buf, mvt_buf)
    key_major = (True, False, True, True, False, False)

    def page_copy(which, b, c, p, slot):
        pid = pt_ref[b, c * PAGES_PER_CHUNK + p]
        off = pl.ds(pl.multiple_of(p * page, page), page)
        dst = bufs[which].at[slot, off] if key_major[which] else bufs[which].at[slot, :, off]
        return pltpu.make_async_copy(pools[which].at[layer, pid], dst, sems.at[slot, which])

    def start_chunk(b, c, slot):
        def body(p, carry):
            for which in range(len(pools)):
                page_copy(which, b, c, p, slot).start()
            return carry
        lax.fori_loop(0, PAGES_PER_CHUNK, body, 0)

    def wait_chunk(b, c, slot):
        def body(p, carry):
            for which in range(len(pools)):
                page_copy(which, b, c, p, slot).wait()
            return carry
        lax.fori_loop(0, PAGES_PER_CHUNK, body, 0)

    def online(s, state):
        m, l, acc = state
        m_new = jnp.maximum(m, jnp.max(s, axis=1, keepdims=True))
        p = jnp.exp(s - m_new)
        alpha = jnp.exp(m - m_new)
        return (m_new, alpha * l + jnp.sum(p, axis=1, keepdims=True), alpha * acc), p.astype(BF16)

    def finish(state, s_new, v_new):
        m, l, acc = state
        m_new = jnp.maximum(m, s_new)
        p_new = jnp.exp(s_new - m_new)
        alpha = jnp.exp(m - m_new)
        return (alpha * acc + _round_bf16(p_new) * _round_bf16(v_new)) / (alpha * l + p_new)

    def new_score(q, k_new):
        return jnp.sum(_round_bf16(q) * _round_bf16(k_new), axis=1, keepdims=True)

    start_chunk(0, 0, 0)

    def sample_body(b, carry):
        qm = qm_ref[b].astype(BF16)
        qd = qd_ref[b].astype(BF16)
        qo = qo_ref[b].astype(BF16)
        fresh = (jnp.full((ATT_ROWS, 1), NEG_INF, F32), jnp.zeros((ATT_ROWS, 1), F32),
                 jnp.zeros((ATT_ROWS, LANES), F32))
        st_m, st_d = fresh, fresh
        blk_m, blk_l, blk_a = [], [], []
        kmean_t = jnp.zeros((LANES, LANES), F32)
        for c in range(n_chunks):
            slot = c % 2
            if c + 1 < n_chunks:
                start_chunk(b, c + 1, 1 - slot)
            else:
                @pl.when(b + 1 < n_samples)
                def _():
                    start_chunk(b + 1, 0, 1 - slot)
            wait_chunk(b, c, slot)
            ckv = ckv_buf[slot].astype(BF16)
            mkt = mkt_buf[slot]
            s_m = (_dot_nt(qm[:, :LANES], ckv)
                   + _dot(qm[:, LANES:LANES + MLA_ROPE], krt_buf[slot].astype(BF16))) * MLA_SCALE
            s_d = _dot_nt(qd, dk_buf[slot].astype(BF16)) + bias_d_ref[:, c * chunk:(c + 1) * chunk]
            s_o = _dot(qo, mkt.astype(BF16)) + bias_o_ref[:, c * chunk:(c + 1) * chunk]
            st_m, p_m = online(s_m, st_m)
            st_d, p_d = online(s_d, st_d)
            p_o = []
            for j in range(blocks_per_chunk):
                lo = j * MOBA_BLOCK
                s = s_o[:, lo:lo + MOBA_BLOCK]
                m_b = jnp.max(s, axis=1, keepdims=True)
                p = jnp.exp(s - m_b)
                blk_m.append(m_b)
                blk_l.append(jnp.sum(p, axis=1, keepdims=True))
                p_o.append(p.astype(BF16))
                mean = jnp.sum(mkt[:, lo:lo + MOBA_BLOCK], axis=1, keepdims=True) * (1.0 / MOBA_BLOCK)
                kmean_t = jnp.where(_lane_iota(kmean_t.shape) == c * blocks_per_chunk + j, mean, kmean_t)
            st_m = (st_m[0], st_m[1], st_m[2] + _dot(p_m, ckv))
            st_d = (st_d[0], st_d[1], st_d[2] + _dot(p_d, dv_buf[slot].astype(BF16)))
            for j in range(blocks_per_chunk):
                lo = j * MOBA_BLOCK
                blk_a.append(_dot_nt(p_o[j], mvt_buf[slot, :, lo:lo + MOBA_BLOCK].astype(BF16)))
        nkm = nkm_ref[b]
        omla_ref[b] = finish(st_m, new_score(qm, nkm) * MLA_SCALE, nkm[:, :LANES])
        odiff_ref[b] = finish(st_d, new_score(qd, ndk_ref[b]) + bnew_d_ref[:, :1], ndv_ref[b])
        sel = _topk_lanes(_dot(qo, kmean_t.astype(BF16)), n_blocks, n_sel)
        s_own = new_score(qo, nmk_ref[b]) + bnew_o_ref[:, :1]
        m_all = s_own
        for j in range(n_blocks):
            m_all = jnp.maximum(m_all, jnp.where(sel[:, j:j + 1] > 0.0, blk_m[j], NEG_INF))
        p_own = jnp.exp(s_own - m_all)
        l_all = p_own
        acc = _round_bf16(p_own) * _round_bf16(nmv_ref[b])
        for j in range(n_blocks):
            w = jnp.where(sel[:, j:j + 1] > 0.0, jnp.exp(blk_m[j] - m_all), 0.0)
            l_all = l_all + w * blk_l[j]
            acc = acc + w * blk_a[j]
        omoba_ref[b] = acc / l_all
        return carry

    lax.fori_loop(0, n_samples, sample_body, 0)


def _sample_attn_call(page_table, qm, qd, qo, nkm, ndk, ndv, nmk, nmv, bias_d, bias_o, bnew_d, bnew_o,
                      caches, layer):
    n_samples, n_pages = page_table.shape
    page = caches[0].shape[2]
    assert n_pages % PAGES_PER_CHUNK == 0 and (n_pages // PAGES_PER_CHUNK) % 2 == 0
    n_chunks = n_pages // PAGES_PER_CHUNK
    chunk = PAGES_PER_CHUNK * page
    assert chunk % MOBA_BLOCK == 0
    n_blocks = n_pages * page // MOBA_BLOCK
    assert n_blocks <= LANES
    n_sel = min(MOBA_TOPK, n_blocks)
    vmem = pl.BlockSpec(memory_space=pltpu.VMEM)
    out = jax.ShapeDtypeStruct((n_samples, ATT_ROWS, LANES), F32)
    buf_shapes = [(2, chunk, LANES), (2, MLA_ROPE, chunk), (2, chunk, LANES), (2, chunk, LANES),
                  (2, LANES, chunk), (2, LANES, chunk)]
    return pl.pallas_call(
        functools.partial(_sample_attn_kernel, layer=layer, n_samples=n_samples, n_chunks=n_chunks, n_sel=n_sel),
        in_specs=[pl.BlockSpec(memory_space=pltpu.SMEM)] + [vmem] * 12 + [pl.BlockSpec(memory_space=pl.ANY)] * 6,
        out_specs=[vmem] * 3,
        out_shape=[out, out, out],
        scratch_shapes=[pltpu.VMEM(s, F32) for s in buf_shapes] + [pltpu.SemaphoreType.DMA((2, len(caches)))],
        compiler_params=pltpu.CompilerParams(vmem_limit_bytes=VMEM_LIMIT),
        name="sample_attn",
    )(page_table, qm, qd, qo, nkm, ndk, ndv, nmk, nmv, bias_d, bias_o, bnew_d, bnew_o, *caches)


def _sample_post_kernel(omla_ref, odiff_ref, omoba_ref, wuv_ref, lq1_ref, lk1_ref, lq2_ref, lk2_ref, subln_ref,
                        mla_o, diff_o, moba_o, *, lam_init):
    _mla_out_proj([omla_ref[:, hd * LANES:(hd + 1) * LANES] for hd in range(MLA_HEADS)], wuv_ref, mla_o)
    lam = _diff_lambda(lq1_ref, lk1_ref, lq2_ref, lk2_ref, lam_init)
    for hd in range(DIFF_HEADS):
        n1 = odiff_ref[:, hd * LANES:(hd + 1) * LANES]
        n2 = odiff_ref[:, (DIFF_HEADS + hd) * LANES:(DIFF_HEADS + hd + 1) * LANES]
        diff_o[:, hd * LANES:(hd + 1) * LANES] = _diff_finish(
            n1, n2, lam, subln_ref[...], lam_init).astype(diff_o.dtype)
    rep = MOBA_HEADS // MOBA_KV_HEADS
    for j in range(rep):
        a = omoba_ref[:, j * LANES:(j + 1) * LANES]
        b = omoba_ref[:, (j + rep) * LANES:(j + rep + 1) * LANES]
        low = _lane_iota(a.shape) < MOBA_HD
        moba_o[:, j * LANES:(j + 1) * LANES] = jnp.where(low, a, b).astype(moba_o.dtype)


def _sample_post_call(omla, odiff, omoba, wuv, lams, subln, lam_init):
    n = omla.shape[0]
    out = jax.ShapeDtypeStruct((n, 4 * LANES), BF16)
    return pl.pallas_call(
        functools.partial(_sample_post_kernel, lam_init=lam_init),
        out_shape=[out, out, out],
        compiler_params=pltpu.CompilerParams(vmem_limit_bytes=VMEM_LIMIT),
        name="sample_post",
    )(omla.reshape(n, -1), odiff.reshape(n, -1), omoba.reshape(n, -1), wuv, *lams, subln)


def _merge_kernel(x_ref, g_ref, omla_ref, odiff_ref, omoba_ref, wg0_ref, wg1_ref, wg2_ref,
                  bg0_ref, bg1_ref, bg2_ref, wb0_ref, wb1_ref, wb2_ref, o_ref, h_scr):
    @pl.when(pl.program_id(1) == 0)
    def _():
        h_scr[...] = _rms(x_ref[...], g_ref[...]).astype(BF16)

    h = h_scr[...]
    merged = None
    for o_in, wg, bg, wb in ((omla_ref, wg0_ref, bg0_ref, wb0_ref), (odiff_ref, wg1_ref, bg1_ref, wb1_ref),
                             (omoba_ref, wg2_ref, bg2_ref, wb2_ref)):
        gate = jax.nn.sigmoid(_dot(h, wg[...]) + bg[...])
        term = gate * _dot(o_in[...], wb[...])
        merged = term if merged is None else merged + term
    o_ref[...] = merged.astype(o_ref.dtype)


def _merge_call(x, g, omla, odiff, omoba, wgate, bgate, wb_mla, wb_diff, wb_moba):
    n, d = x.shape
    tm = _tile(n, 512)
    tn = _tile(d, 512)
    nd = d // tn
    row = lambda w: pl.BlockSpec((tm, w), lambda i, j: (i, 0))
    gate_w = [pl.BlockSpec((d, tn), functools.partial(lambda i, j, br: (0, br * nd + j), br=br)) for br in range(3)]
    gate_b = [pl.BlockSpec((1, tn), functools.partial(lambda i, j, br: (0, br * nd + j), br=br)) for br in range(3)]
    branch_w = [pl.BlockSpec((w.shape[0], tn), lambda i, j: (0, j)) for w in (wb_mla, wb_diff, wb_moba)]
    return pl.pallas_call(
        _merge_kernel,
        grid=(n // tm, nd),
        in_specs=[row(d), pl.BlockSpec(g.shape, lambda i, j: (0, 0)),
                  row(omla.shape[1]), row(odiff.shape[1]), row(omoba.shape[1])] + gate_w + gate_b + branch_w,
        out_specs=pl.BlockSpec((tm, tn), lambda i, j: (i, j)),
        out_shape=jax.ShapeDtypeStruct((n, d), BF16),
        scratch_shapes=[pltpu.VMEM((tm, d), BF16)],
        compiler_params=_params(("parallel", "arbitrary")),
        name="gated_merge",
    )(x, g, omla, odiff, omoba, wgate, wgate, wgate, bgate, bgate, bgate, wb_mla, wb_diff, wb_moba)


def _out_proj_kernel(x_ref, merged_ref, wout_ref, gpost_ref, gmlp_ref, x1_ref, hn_ref):
    x1 = x_ref[...] + _rms(_dot(merged_ref[...], wout_ref[...]), gpost_ref[...])
    x1_ref[...] = x1
    hn_ref[...] = _rms(x1, gmlp_ref[...]).astype(hn_ref.dtype)


def _out_proj_call(x, merged, wout, gpost, gmlp):
    n, d = x.shape
    tm = _tile(n, 256)
    row = pl.BlockSpec((tm, d), lambda i: (i, 0))
    full = lambda a: pl.BlockSpec(a.shape, lambda i: (0, 0))
    return pl.pallas_call(
        _out_proj_kernel,
        grid=(n // tm,),
        in_specs=[row, row, full(wout), full(gpost), full(gmlp)],
        out_specs=[row, row],
        out_shape=[jax.ShapeDtypeStruct((n, d), F32), jax.ShapeDtypeStruct((n, d), BF16)],
        compiler_params=_params(("parallel",)),
        name="out_proj",
    )(x, merged, wout, gpost, gmlp)


def _mlp_kernel(hn_ref, wup_ref, wdown_ref, x1_ref, gpost_ref, x2_ref, acc_scr):
    k = pl.program_id(1)

    @pl.when(k == 0)
    def _():
        acc_scr[...] = jnp.zeros(acc_scr.shape, F32)

    u = jnp.square(jnp.maximum(_dot(hn_ref[...], wup_ref[...]), 0.0)).astype(BF16)
    acc_scr[...] += _dot(u, wdown_ref[...])

    @pl.when(k == pl.num_programs(1) - 1)
    def _():
        x2_ref[...] = x1_ref[...] + _rms(acc_scr[...], gpost_ref[...])


def _mlp_call(hn, wup, wdown, x1, gpost):
    n, d = x1.shape
    dff = wup.shape[1]
    tm = _tile(n, 512)
    tf = _tile(dff, 1024)
    row = pl.BlockSpec((tm, d), lambda i, k: (i, 0))
    return pl.pallas_call(
        _mlp_kernel,
        grid=(n // tm, dff // tf),
        in_specs=[row, pl.BlockSpec((d, tf), lambda i, k: (0, k)), pl.BlockSpec((tf, d), lambda i, k: (k, 0)),
                  row, pl.BlockSpec(gpost.shape, lambda i, k: (0, 0))],
        out_specs=row,
        out_shape=jax.ShapeDtypeStruct((n, d), F32),
        scratch_shapes=[pltpu.VMEM((tm, d), F32)],
        compiler_params=_params(("parallel", "arbitrary")),
        name="relu2_mlp",
    )(hn, wup, wdown, x1, gpost)


def _ple_kernel(x2_ref, p_ref, gple_ref, wpg_ref, wple_ref, x3_ref):
    x2 = x2_ref[...]
    gate = jax.nn.sigmoid(_dot(_rms(x2, gple_ref[...]).astype(BF16), wpg_ref[...]))
    x3_ref[...] = x2 + gate * _dot(p_ref[...].astype(BF16), wple_ref[...])


def _ple_call(x2, p, gple, wpg, wple):
    n, d = x2.shape
    tm = _tile(n, 256)
    row = lambda w: pl.BlockSpec((tm, w), lambda i: (i, 0))
    full = lambda a: pl.BlockSpec(a.shape, lambda i: (0, 0))
    return pl.pallas_call(
        _ple_kernel,
        grid=(n // tm,),
        in_specs=[row(d), row(p.shape[1]), full(gple), full(wpg), full(wple)],
        out_specs=row(d),
        out_shape=jax.ShapeDtypeStruct((n, d), F32),
        compiler_params=_params(("parallel",)),
        name="ple_gate",
    )(x2, p, gple, wpg, wple)


def _pad_cols(w, width):
    return jnp.pad(w, ((0, 0), (0, width - w.shape[1])))


def _rot_cols(w):
    half = w.shape[1] // 2
    return jnp.concatenate([-w[:, half:], w[:, :half]], axis=1)


def _prep_layer(w_in, w_uq, w_ukv, wb_moba):
    sizes = (MLA_Q_LORA, MLA_KV_LORA, MLA_ROPE, DIFF_HEADS * 2 * DIFF_HD, 2 * DIFF_HD, 2 * DIFF_HD,
             MOBA_HEADS * MOBA_HD, MOBA_KV_HEADS * MOBA_HD, MOBA_KV_HEADS * MOBA_HD)
    offs = [0]
    for s in sizes:
        offs.append(offs[-1] + s)
    cq, ckv, kr, dq, dk, dv, mq, mk, mv = [w_in[:, offs[i]:offs[i + 1]] for i in range(len(sizes))]
    win = jnp.concatenate([cq, ckv, _pad_cols(kr, LANES), _pad_cols(_rot_cols(kr), LANES),
                           dq, dk, dv, mq, mk, mv], axis=1).astype(BF16)
    assert win.shape[1] == _C_END
    per_head = MLA_NOPE + MLA_ROPE
    nope, rope, rot = [], [], []
    for hd in range(MLA_HEADS):
        blk = w_uq[:, hd * per_head:(hd + 1) * per_head]
        nope.append(_pad_cols(blk[:, :MLA_NOPE], LANES))
        rope.append(_pad_cols(blk[:, MLA_NOPE:], LANES))
        rot.append(_pad_cols(_rot_cols(blk[:, MLA_NOPE:]), LANES))
    wq = jnp.concatenate(nope + rope + rot, axis=1).astype(BF16)
    w3 = w_ukv.reshape(MLA_KV_LORA, MLA_HEADS, MLA_NOPE + MLA_V)
    wuk = jnp.pad(jnp.transpose(w3[..., :MLA_NOPE], (1, 2, 0)), ((0, 0), (0, LANES - MLA_NOPE), (0, 0))).astype(BF16)
    wuv_h = jnp.transpose(w3[..., MLA_NOPE:], (1, 0, 2))
    zero = jnp.zeros_like(wuv_h[0])
    wuv = jnp.stack([jnp.concatenate([jnp.concatenate([wuv_h[2 * j], zero], axis=1),
                                      jnp.concatenate([zero, wuv_h[2 * j + 1]], axis=1)], axis=0)
                     for j in range(MLA_HEADS // 2)]).astype(BF16)
    rep = MOBA_HEADS // MOBA_KV_HEADS
    order = [hd for j in range(rep) for hd in (j, j + rep)]
    wbm = jnp.concatenate([wb_moba[hd * MOBA_HD:(hd + 1) * MOBA_HD] for hd in order], axis=0).astype(BF16)
    return win, wq, wuk, wuv, wbm


def _rope_tables(pos):
    half = MLA_ROPE // 2
    inv = ROPE_THETA ** (-jnp.arange(half, dtype=F32) / half)
    ang = pos.astype(F32)[:, None] * inv[None, :]
    cos = jnp.cos(ang)
    sin = jnp.sin(ang)
    return (_pad_cols(jnp.concatenate([cos, cos], axis=1), LANES),
            _pad_cols(jnp.concatenate([sin, sin], axis=1), LANES))


def _tile_bias(per_head):
    h, two, tk, tq = per_head.shape
    return jnp.transpose(per_head, (1, 2, 0, 3)).reshape(two, tk, h * tq)


def kernel(x_prompt, x_sample, cache_mla_ckv, cache_mla_krope, cache_diff_k, cache_diff_v, cache_moba_k,
           cache_moba_v, page_table, p_prompt, p_sample, rel_bias_table, norm_pre_mix, norm_post_mix,
           norm_pre_mlp, norm_post_mlp, w_in, mla_q_norm, mla_kv_norm, mla_w_uq, mla_w_ukv, diff_lambda_q1,
           diff_lambda_k1, diff_lambda_q2, diff_lambda_k2, diff_subln, w_branch_mla, w_branch_diff,
           w_branch_moba, w_gate, b_gate, w_out, w_up, w_down, ple_norm, w_ple_gate, w_ple):
    depth = w_in.shape[0]
    bsz, t, d = x_prompt.shape
    n_s = x_sample.shape[0]
    tb = ATT_BLOCK
    assert x_sample.shape[1] == 1 and t % tb == 0 and n_s <= tb
    n_pool, page = cache_mla_ckv.shape[1], cache_mla_ckv.shape[2]
    past = page_table.shape[1] * page
    caches = (cache_mla_ckv,
              jnp.transpose(cache_mla_krope, (0, 1, 3, 2)),
              cache_diff_k.reshape(depth, n_pool, page, LANES), cache_diff_v.reshape(depth, n_pool, page, LANES),
              jnp.transpose(cache_moba_k, (0, 1, 3, 4, 2)).reshape(depth, n_pool, LANES, page),
              jnp.transpose(cache_moba_v, (0, 1, 3, 4, 2)).reshape(depth, n_pool, LANES, page))

    cos_p, sin_p = _rope_tables(jnp.arange(t, dtype=jnp.int32))
    cos_p, sin_p = jnp.tile(cos_p, (bsz, 1)), jnp.tile(sin_p, (bsz, 1))
    cos_s, sin_s = _rope_tables(jnp.full((n_s,), past, dtype=jnp.int32))
    ii = jnp.arange(tb, dtype=jnp.int32)
    delta = ii[None, :] - ii[:, None]
    dist = jnp.concatenate([jnp.maximum(delta, 0), delta + tb], axis=0)
    btile = _bias_gather(rel_bias_table, _t5_bucket(dist)).reshape(-1, 2, tb, tb)
    far_bucket = N_BUCKETS - 1
    assert tb + 1 >= MAX_DISTANCE
    bfar = jnp.repeat(rel_bias_table[far_bucket], tb)[None, :]
    n_d = DIFF_HEADS * tb
    btile_d = _tile_bias(jnp.concatenate([btile[:DIFF_HEADS], btile[:DIFF_HEADS]], axis=0))
    btile_o = _tile_bias(btile[DIFF_HEADS:])
    bfar_d = jnp.concatenate([bfar[:, :n_d], bfar[:, :n_d]], axis=1)
    bfar_o = bfar[:, n_d:]
    dist_s = (past - jnp.arange(past, dtype=jnp.int32))[None, :]
    bias_keys = _bias_gather(rel_bias_table, _t5_bucket(dist_s))[:, 0, :]
    bias_d = jnp.concatenate([bias_keys[:DIFF_HEADS], bias_keys[:DIFF_HEADS]], axis=0)
    bias_o = bias_keys[DIFF_HEADS:]
    bnew = jnp.broadcast_to(rel_bias_table[0][:, None], (DIFF_HEADS + MOBA_HEADS, LANES))
    bnew_d = jnp.concatenate([bnew[:DIFF_HEADS], bnew[:DIFF_HEADS]], axis=0)
    bnew_o = bnew[DIFF_HEADS:]

    xp = x_prompt.reshape(bsz * t, d)
    xs = x_sample.reshape(n_s, d)
    rows_p, rows_s = [], []
    r2 = lambda a: a.reshape(1, -1)
    nblk = t // MOBA_BLOCK
    nbp = -(-nblk // SUBLANES) * SUBLANES
    for i in range(depth):
        lam_init = 0.8 - 0.6 * math.exp(-0.3 * i)
        win, wq, wuk, wuv, wbm = _prep_layer(w_in[i], mla_w_uq[i], mla_w_ukv[i], w_branch_moba[i])
        lams = [r2(diff_lambda_q1[i]), r2(diff_lambda_k1[i]), r2(diff_lambda_q2[i]), r2(diff_lambda_k2[i])]
        subln = r2(diff_subln[i])
        wgate, bgate = w_gate[i].astype(BF16), r2(b_gate[i])
        wb_mla, wb_diff = w_branch_mla[i].astype(BF16), w_branch_diff[i].astype(BF16)
        wout, wup, wdown = w_out[i].astype(BF16), w_up[i].astype(BF16), w_down[i].astype(BF16)
        wpg, wple = w_ple_gate[i].astype(BF16), w_ple[i].astype(BF16)
        pre_w = (r2(norm_pre_mix[i]), win, r2(mla_q_norm[i]), r2(mla_kv_norm[i]), wq, wuk)

        def tail(x, omla, odiff, omoba, p):
            merged = _merge_call(x, r2(norm_pre_mix[i]), omla, odiff, omoba, wgate, bgate, wb_mla, wb_diff, wbm)
            x1, hn = _out_proj_call(x, merged, wout, r2(norm_post_mix[i]), r2(norm_pre_mlp[i]))
            x2 = _mlp_call(hn, wup, wdown, x1, r2(norm_post_mlp[i]))
            return _ple_call(x2, p, r2(ple_norm[i]), wpg, wple)

        (ckv, kr, dk, dv, mk, mv, kcat, ckvt, dkb, dvt, mkb, mvt, qm, qd, qo, kmean) = _pre_call(
            xp, *pre_w, cos_p, sin_p, with_kmean=True)
        rows_p.append((ckv, kr, dk, dv, mk, mv))
        kmean_pad = jnp.pad(kmean[:, 0, :].reshape(bsz, nblk, LANES), ((0, 0), (0, nbp - nblk), (0, 0)))
        flat = lambda q: q.reshape(-1, q.shape[-1])
        omla = _mla_attn_call(flat(qm), kcat, ckvt, wuv, bsz, t)
        odiff = _diff_attn_call(flat(qd), dkb, dvt, btile_d, bfar_d, lams, subln, bsz, t, lam_init)
        omoba = _moba_attn_call(flat(qo), mkb, mvt, kmean_pad, btile_o, bfar_o, bsz, t)
        xp = tail(xp, omla, odiff, omoba, p_prompt[i].reshape(bsz * t, -1))

        (ckv, kr, dk, dv, mk, mv, _, _, _, _, _, _, qm, qd, qo, _) = _pre_call(
            xs, *pre_w, cos_s, sin_s, with_kmean=False)
        rows_s.append((ckv, kr, dk, dv, mk, mv))
        per_sample = lambda q: jnp.transpose(q[0], (1, 0, 2)).astype(F32)
        nkm = jnp.concatenate([ckv, _pad_cols(kr, LANES)], axis=1)[:, None, :]
        a_mla, a_diff, a_moba = _sample_attn_call(
            page_table, per_sample(qm), per_sample(qd), per_sample(qo), nkm,
            dk[:, None, :], dv[:, None, :], mk[:, None, :], mv[:, None, :],
            bias_d, bias_o, bnew_d, bnew_o, caches, i)
        omla, odiff, omoba = _sample_post_call(a_mla, a_diff, a_moba, wuv, lams, subln, lam_init)
        xs = tail(xs, omla, odiff, omoba, p_sample[i].reshape(n_s, -1))

    def stack(rows, lead):
        cols = list(zip(*rows))
        shapes = [(MLA_KV_LORA,), (MLA_ROPE,), (1, 2 * DIFF_HD), (1, 2 * DIFF_HD),
                  (MOBA_KV_HEADS, MOBA_HD), (MOBA_KV_HEADS, MOBA_HD)]
        return tuple(jnp.stack(c, axis=0).reshape((depth,) + lead + s) for c, s in zip(cols, shapes))

    return ((xp.reshape(bsz, t, d), xs.reshape(n_s, 1, d)) + stack(rows_p, (bsz, t)) + stack(rows_s, (n_s, 1)))
```

```python
import functools
import math

import jax
import jax.numpy as jnp
from jax import lax
from jax.experimental import pallas as pl
from jax.experimental.pallas import tpu as pltpu

MLA_HEADS = 8
MLA_Q_LORA = 384
MLA_KV_LORA = 128
MLA_NOPE = 64
MLA_ROPE = 32
MLA_V = 64
MLA_SCALE = (MLA_NOPE + MLA_ROPE) ** -0.5
ROPE_THETA = 10000.0
DIFF_HEADS = 4
DIFF_HD = 64
DIFF_SCALE = DIFF_HD ** -0.5
MOBA_HEADS = 8
MOBA_KV_HEADS = 2
MOBA_HD = 64
MOBA_BLOCK = 256
MOBA_TOPK = 3
MOBA_SCALE = MOBA_HD ** -0.5
N_BUCKETS = 32
MAX_DISTANCE = 128
N_BRANCH = 3
EPS = 1e-6
NEG_INF = -1e30

LANES = 128
SUBLANES = 8
ATT_BLOCK = 256
ATT_ROWS = 8
FLASH_SPLIT = 4
FLASH_SPLIT_NEAR = 2
LOG2E = math.log2(math.e)
PAGES_PER_CHUNK = 16
VMEM_LIMIT = 48 * 1024 * 1024

F32 = jnp.float32
BF16 = jnp.bfloat16

_C_CQ = 0
_C_CKV = _C_CQ + MLA_Q_LORA
_C_KR = _C_CKV + LANES
_C_KRR = _C_KR + LANES
_C_DQ = _C_KRR + LANES
_C_DK = _C_DQ + DIFF_HEADS * LANES
_C_DV = _C_DK + LANES
_C_MQ = _C_DV + LANES
_C_MK = _C_MQ + MOBA_HEADS * MOBA_HD
_C_MV = _C_MK + LANES
_C_END = _C_MV + LANES


def _rms(x, g):
    return x * lax.rsqrt(jnp.mean(x * x, axis=-1, keepdims=True) + EPS) * g


def _dot(a, b):
    return jnp.dot(a, b, preferred_element_type=F32)


def _dot_nt(a, b):
    return lax.dot_general(a, b, (((1,), (1,)), ((), ())), preferred_element_type=F32)


def _lane_iota(shape):
    return lax.broadcasted_iota(jnp.int32, shape, len(shape) - 1)


def _row_iota(shape):
    return lax.broadcasted_iota(jnp.int32, shape, 0)


def _params(sem):
    return pltpu.CompilerParams(dimension_semantics=sem, vmem_limit_bytes=VMEM_LIMIT)


def _tile(n, want):
    t = min(n, want)
    assert n % t == 0, (n, t)
    return t


def _t5_bucket(n):
    max_exact = N_BUCKETS // 2
    nf = jnp.maximum(n, max_exact).astype(F32)
    large = max_exact + (jnp.log(nf / max_exact)
                         * ((N_BUCKETS - max_exact) / math.log(MAX_DISTANCE / max_exact))).astype(jnp.int32)
    return jnp.where(n < max_exact, n, jnp.minimum(large, N_BUCKETS - 1))


def _bias_gather_kernel(table_ref, bucket_ref, out_ref):
    h = pl.program_id(0)
    b = bucket_ref[...]
    acc = jnp.zeros(b.shape, F32)
    for n in range(N_BUCKETS):
        acc = jnp.where(b == n, table_ref[n, h], acc)
    out_ref[0] = acc


def _bias_gather(table, buckets):
    n_heads = table.shape[1]
    r, c = buckets.shape
    return pl.pallas_call(
        _bias_gather_kernel,
        grid=(n_heads,),
        in_specs=[pl.BlockSpec(memory_space=pltpu.SMEM),
                  pl.BlockSpec((r, c), lambda h: (0, 0))],
        out_specs=pl.BlockSpec((1, r, c), lambda h: (h, 0, 0)),
        out_shape=jax.ShapeDtypeStruct((n_heads, r, c), F32),
        compiler_params=_params(("arbitrary",)),
        name="bias_gather",
    )(table, buckets)


def _pre_kernel(x_ref, g_ref, win_ref, qn_ref, kvn_ref, wq_ref, wuk_ref, cos_ref, sin_ref,
                ckv_o, kr_o, dk_o, dv_o, mk_o, mv_o,
                kcat_o, ckvt_o, dkb_o, dvt_o, mkb_o, mvt_o, qm_o, qd_o, qo_o, kmean_o, *, with_kmean):
    h = _rms(x_ref[...], g_ref[...]).astype(BF16)
    proj = _dot(h, win_ref[...])
    cos = cos_ref[...]
    sin = sin_ref[...]
    ckv_n = _rms(proj[:, _C_CKV:_C_KR], kvn_ref[...])
    kr = proj[:, _C_KR:_C_KRR] * cos + proj[:, _C_KRR:_C_DQ] * sin
    dk = proj[:, _C_DK:_C_DV]
    dv = proj[:, _C_DV:_C_MQ]
    mk = proj[:, _C_MK:_C_MV]
    mv = proj[:, _C_MV:_C_END]
    ckv_o[...] = ckv_n
    kr_o[...] = kr[:, :MLA_ROPE]
    dk_o[...] = dk
    dv_o[...] = dv
    mk_o[...] = mk
    mv_o[...] = mv
    kcat_o[:, :LANES] = ckv_n.astype(BF16)
    kcat_o[:, LANES:] = kr.astype(BF16)
    ckvt_o[...] = ckv_n.T.astype(BF16)
    dkb_o[...] = dk.astype(BF16)
    dvt_o[...] = dv.T.astype(BF16)
    mkb_o[...] = mk.astype(BF16)
    mvt_o[...] = mv.T.astype(BF16)
    low = _lane_iota(dk.shape) < DIFF_HD
    for hd in range(DIFF_HEADS):
        qb = proj[:, _C_DQ + hd * LANES:_C_DQ + (hd + 1) * LANES] * DIFF_SCALE
        qd_o[0, hd] = jnp.where(low, qb, 0.0).astype(BF16)
        qd_o[0, DIFF_HEADS + hd] = jnp.where(low, 0.0, qb).astype(BF16)
    rep = MOBA_HEADS // MOBA_KV_HEADS
    for j in range(MOBA_HEADS // 2):
        qb = proj[:, _C_MQ + j * LANES:_C_MQ + (j + 1) * LANES] * MOBA_SCALE
        swapped = pltpu.roll(qb, MOBA_HD, 1)
        for half in range(2):
            hd = 2 * j + half
            src = qb if hd // rep == half else swapped
            qo_o[0, hd] = (jnp.where(low, src, 0.0) if hd // rep == 0 else jnp.where(low, 0.0, src)).astype(BF16)
    if with_kmean:
        nblk = mk.shape[0] // MOBA_BLOCK
        km = jnp.sum(mk.reshape(nblk, MOBA_BLOCK, LANES), axis=1) * (1.0 / MOBA_BLOCK)
        kmean_o[...] = jnp.broadcast_to(km[:, None, :], kmean_o.shape)
    else:
        kmean_o[...] = jnp.zeros(kmean_o.shape, F32)
    cq = _rms(proj[:, _C_CQ:_C_CKV], qn_ref[...]).astype(BF16)
    q_all = _dot(cq, wq_ref[...])
    nq = MLA_HEADS * LANES
    for hd in range(MLA_HEADS):
        q_nope = q_all[:, hd * LANES:(hd + 1) * LANES].astype(BF16)
        q_lat = _dot(q_nope, wuk_ref[hd])
        q_rope = (q_all[:, nq + hd * LANES:nq + (hd + 1) * LANES] * cos
                  + q_all[:, 2 * nq + hd * LANES:2 * nq + (hd + 1) * LANES] * sin)
        qm_o[0, hd, :, :LANES] = q_lat.astype(BF16)
        qm_o[0, hd, :, LANES:] = q_rope.astype(BF16)


def _pre_call(x, g, win, qn, kvn, wq, wuk, cos, sin, *, with_kmean):
    n, d = x.shape
    tm = _tile(n, ATT_BLOCK)
    nt = n // tm
    nkm = max(tm // MOBA_BLOCK, 1)
    row = lambda w: pl.BlockSpec((tm, w), lambda i: (i, 0))
    col = pl.BlockSpec((LANES, tm), lambda i: (0, i))
    grp = lambda w: pl.BlockSpec((1, ATT_ROWS, tm, w), lambda i: (i, 0, 0, 0))
    full = lambda a: pl.BlockSpec(a.shape, lambda i: (0,) * a.ndim)
    sds = jax.ShapeDtypeStruct
    outs = [
        (sds((n, LANES), F32), row(LANES)), (sds((n, MLA_ROPE), F32), row(MLA_ROPE)),
        (sds((n, LANES), F32), row(LANES)), (sds((n, LANES), F32), row(LANES)),
        (sds((n, LANES), F32), row(LANES)), (sds((n, LANES), F32), row(LANES)),
        (sds((n, 2 * LANES), BF16), row(2 * LANES)), (sds((LANES, n), BF16), col),
        (sds((n, LANES), BF16), row(LANES)), (sds((LANES, n), BF16), col),
        (sds((n, LANES), BF16), row(LANES)), (sds((LANES, n), BF16), col),
        (sds((nt, ATT_ROWS, tm, 2 * LANES), BF16), grp(2 * LANES)),
        (sds((nt, ATT_ROWS, tm, LANES), BF16), grp(LANES)),
        (sds((nt, ATT_ROWS, tm, LANES), BF16), grp(LANES)),
        (sds((nt * nkm, SUBLANES, LANES), F32), pl.BlockSpec((nkm, SUBLANES, LANES), lambda i: (i, 0, 0))),
    ]
    return pl.pallas_call(
        functools.partial(_pre_kernel, with_kmean=with_kmean),
        grid=(nt,),
        in_specs=[row(d), full(g), full(win), full(qn), full(kvn), full(wq), full(wuk), row(LANES), row(LANES)],
        out_specs=[o[1] for o in outs],
        out_shape=[o[0] for o in outs],
        compiler_params=_params(("parallel",)),
        name="pre_mix",
    )(x, g, win, qn, kvn, wq, wuk, cos, sin)


def _init_state(m_scr, l_scr, acc_scr):
    m_scr[...] = jnp.full(m_scr.shape, NEG_INF, F32)
    l_scr[...] = jnp.zeros(l_scr.shape, F32)
    acc_scr[...] = jnp.zeros(acc_scr.shape, F32)


def _flash_step(q_ref, k, vt, m_scr, l_scr, acc_scr, adjust, exp=jnp.exp, n_split=FLASH_SPLIT,
                column_terms=None):
    nq = q_ref.shape[0]
    step = nq // n_split
    parts = [(g * step, (g + 1) * step) for g in range(n_split)]
    scores = [_dot_nt(k, q_ref[lo:hi, :]) for lo, hi in parts]
    for (lo, hi), s in zip(parts, scores):
        s = adjust(s, lo, hi)
        m_prev = m_scr[:, lo:hi]
        m_cur = jnp.max(s, axis=0, keepdims=True)
        bias, valid = (None, None) if column_terms is None else column_terms(lo, hi)
        if bias is not None:
            m_cur = m_cur + bias
        if valid is not None:
            m_cur = jnp.where(valid, m_cur, NEG_INF)
        m_next = jnp.maximum(m_prev, m_cur)
        shift = m_next if bias is None else m_next - bias
        if valid is not None:
            shift = jnp.where(valid, shift, -NEG_INF)
        p = exp(s - shift)
        alpha = exp(m_prev - m_next)
        l_scr[:, lo:hi] = alpha * l_scr[:, lo:hi] + jnp.sum(p, axis=0, keepdims=True)
        m_scr[:, lo:hi] = m_next
        acc_scr[:, lo:hi] = acc_scr[:, lo:hi] * alpha + _dot(vt, p.astype(BF16))


def _causal(s, tq):
    return jnp.where(_row_iota(s.shape) <= (_lane_iota(s.shape) & (tq - 1)), s, NEG_INF)


def _row_out(acc_scr, l_scr, r, tq):
    return (acc_scr[:, r * tq:(r + 1) * tq] / l_scr[:, r * tq:(r + 1) * tq]).T


def _mla_out_proj(o_heads, wuv_ref, o_ref):
    for j in range(MLA_HEADS // 2):
        pair = jnp.concatenate([o_heads[2 * j], o_heads[2 * j + 1]], axis=1).astype(BF16)
        o_ref[:, j * LANES:(j + 1) * LANES] = _dot(pair, wuv_ref[j]).astype(o_ref.dtype)


def _tile_pair(qt_ref, kt_ref):
    s = pl.program_id(1)
    return qt_ref[s], kt_ref[s]


def _mla_attn_kernel(qt_ref, kt_ref, q_ref, k_ref, vt_ref, wuv_ref, o_ref, m_scr, l_scr, acc_scr, *, tq):
    qi, ki = _tile_pair(qt_ref, kt_ref)

    @pl.when(ki == 0)
    def _():
        _init_state(m_scr, l_scr, acc_scr)

    def step(diagonal):
        def adjust(s, lo, hi):
            s = s * (MLA_SCALE * LOG2E)
            return _causal(s, tq) if diagonal else s
        _flash_step(q_ref, k_ref[...], vt_ref[...], m_scr, l_scr, acc_scr, adjust, exp=jnp.exp2)

    pl.when(ki < qi)(lambda: step(False))

    @pl.when(ki == qi)
    def _():
        step(True)
        _mla_out_proj([_row_out(acc_scr, l_scr, hd, tq) for hd in range(MLA_HEADS)], wuv_ref, o_ref)


def _attn_call(body, name, bsz, t, q, k, vt, extra_inputs, extra_specs, extra_scratch=()):
    tb = ATT_BLOCK
    nq = t // tb
    rows = ATT_ROWS * tb
    pairs = [(qi, ki) for qi in range(nq) for ki in range(qi + 1)]
    qtab = jnp.asarray([p[0] for p in pairs], jnp.int32)
    ktab = jnp.asarray([p[1] for p in pairs], jnp.int32)
    specs = [pl.BlockSpec((rows, q.shape[1]), lambda b, s, qt, kt: (b * nq + qt[s], 0)),
             pl.BlockSpec((tb, k.shape[1]), lambda b, s, qt, kt: (b * nq + kt[s], 0)),
             pl.BlockSpec((LANES, tb), lambda b, s, qt, kt: (0, b * nq + kt[s]))]
    scratch = [pltpu.VMEM((1, rows), F32), pltpu.VMEM((1, rows), F32), pltpu.VMEM((LANES, rows), F32)]
    return pl.pallas_call(
        body,
        grid_spec=pltpu.PrefetchScalarGridSpec(
            num_scalar_prefetch=2,
            grid=(bsz, len(pairs)),
            in_specs=specs + list(extra_specs),
            out_specs=pl.BlockSpec((tb, 4 * LANES), lambda b, s, qt, kt: (b * nq + qt[s], 0)),
            scratch_shapes=scratch + list(extra_scratch)),
        out_shape=jax.ShapeDtypeStruct((bsz * t, 4 * LANES), BF16),
        compiler_params=_params(("parallel", "arbitrary")),
        name=name,
    )(qtab, ktab, q, k, vt, *extra_inputs)


def _const_spec(a):
    return pl.BlockSpec(a.shape, lambda b, s, qt, kt: (0,) * a.ndim)


def _mla_attn_call(q, kcat, ckvt, wuv, bsz, t):
    return _attn_call(functools.partial(_mla_attn_kernel, tq=ATT_BLOCK), "mla_attn", bsz, t, q, kcat, ckvt,
                      [wuv], [_const_spec(wuv)])


def _diff_lambda(lq1_ref, lk1_ref, lq2_ref, lk2_ref, lam_init):
    return (jnp.exp(jnp.sum(lq1_ref[...] * lk1_ref[...], axis=1, keepdims=True))
            - jnp.exp(jnp.sum(lq2_ref[...] * lk2_ref[...], axis=1, keepdims=True)) + lam_init)


def _diff_finish(n1, n2, lam, subln, lam_init):
    return _rms(n1 - lam * n2, subln) * (1.0 - lam_init)


def _biased_step(q_ref, k_ref, vt_ref, btile_ref, bfar_ref, m_scr, l_scr, acc_scr, rel, tq, chosen=None):
    def adjust(s, lo, hi):
        if rel is None:
            return s
        s = s + btile_ref[rel, :, lo:hi]
        return _causal(s, tq) if rel == 0 else s

    def column_terms(lo, hi):
        bias = bfar_ref[:, lo:hi] if rel is None else None
        valid = chosen[:, lo:hi] > 0.0 if (chosen is not None and rel != 0) else None
        return bias, valid

    _flash_step(q_ref, k_ref[...], vt_ref[...], m_scr, l_scr, acc_scr, adjust, column_terms=column_terms,
                n_split=FLASH_SPLIT if rel is None else FLASH_SPLIT_NEAR)


def _diff_attn_kernel(qt_ref, kt_ref, q_ref, k_ref, vt_ref, btile_ref, bfar_ref, lq1_ref, lk1_ref, lq2_ref, lk2_ref,
                      subln_ref, o_ref, m_scr, l_scr, acc_scr, *, lam_init, tq):
    qi, ki = _tile_pair(qt_ref, kt_ref)

    @pl.when(ki == 0)
    def _():
        _init_state(m_scr, l_scr, acc_scr)

    step = functools.partial(_biased_step, q_ref, k_ref, vt_ref, btile_ref, bfar_ref, m_scr, l_scr, acc_scr, tq=tq)
    pl.when(ki < qi - 1)(lambda: step(rel=None))
    pl.when(ki == qi - 1)(lambda: step(rel=1))

    @pl.when(ki == qi)
    def _():
        step(rel=0)
        lam = _diff_lambda(lq1_ref, lk1_ref, lq2_ref, lk2_ref, lam_init)
        for hd in range(DIFF_HEADS):
            o_ref[:, hd * LANES:(hd + 1) * LANES] = _diff_finish(
                _row_out(acc_scr, l_scr, hd, tq), _row_out(acc_scr, l_scr, DIFF_HEADS + hd, tq),
                lam, subln_ref[...], lam_init).astype(o_ref.dtype)


def _diff_attn_call(q, dkb, dvt, btile, bfar, lams, subln, bsz, t, lam_init):
    extra = [btile, bfar, *lams, subln]
    return _attn_call(functools.partial(_diff_attn_kernel, lam_init=lam_init, tq=ATT_BLOCK), "diff_attn",
                      bsz, t, q, dkb, dvt, extra, [_const_spec(a) for a in extra])


def _topk_rows(scores, n_valid, n_blocks, n_sel):
    row = _row_iota(scores.shape)
    sm = jnp.where(row < n_valid, scores, NEG_INF)
    rank = jnp.zeros(scores.shape, F32)
    for j in range(n_blocks):
        r = sm[j:j + 1, :]
        rank = rank + jnp.where(r > sm, 1.0, jnp.where(r == sm, jnp.where(row > j, 1.0, 0.0), 0.0))
    return jnp.where(rank < n_sel, jnp.where(row < n_valid, 1.0, 0.0), 0.0)


def _topk_lanes(scores, n_blocks, n_sel):
    lane = _lane_iota(scores.shape)
    sm = jnp.where(lane < n_blocks, scores, NEG_INF)
    rank = jnp.zeros(scores.shape, F32)
    for j in range(n_blocks):
        c = sm[:, j:j + 1]
        rank = rank + jnp.where(c > sm, 1.0, jnp.where(c == sm, jnp.where(lane > j, 1.0, 0.0), 0.0))
    return jnp.where(rank < n_sel, jnp.where(lane < n_blocks, 1.0, 0.0), 0.0)


def _moba_attn_kernel(qt_ref, kt_ref, q_ref, k_ref, vt_ref, kmean_ref, btile_ref, bfar_ref, o_ref,
                      m_scr, l_scr, acc_scr, sel_scr, *, n_blocks, n_sel, tq):
    qi, ki = _tile_pair(qt_ref, kt_ref)
    rep = MOBA_HEADS // MOBA_KV_HEADS

    @pl.when(ki == 0)
    def _():
        _init_state(m_scr, l_scr, acc_scr)
        blk_s = _dot_nt(kmean_ref[0].astype(BF16), q_ref[...])
        sel_scr[...] = _topk_rows(blk_s, qi, n_blocks, n_sel)

    def step(rel):
        sel = sel_scr[...]
        chosen = jnp.max(jnp.where(_row_iota(sel.shape) == ki, sel, 0.0), axis=0, keepdims=True)
        _biased_step(q_ref, k_ref, vt_ref, btile_ref, bfar_ref, m_scr, l_scr, acc_scr, rel, tq, chosen)

    pl.when(ki < qi - 1)(lambda: step(None))
    pl.when(ki == qi - 1)(lambda: step(1))

    @pl.when(ki == qi)
    def _():
        step(0)
        for j in range(rep):
            a = _row_out(acc_scr, l_scr, j, tq)
            b = _row_out(acc_scr, l_scr, j + rep, tq)
            low = _lane_iota(a.shape) < MOBA_HD
            o_ref[:, j * LANES:(j + 1) * LANES] = jnp.where(low, a, b).astype(o_ref.dtype)


def _moba_attn_call(q, mkb, mvt, kmean_pad, btile, bfar, bsz, t):
    n_blocks = t // MOBA_BLOCK
    n_sel = min(MOBA_TOPK, n_blocks - 1)
    nbp = kmean_pad.shape[1]
    return _attn_call(functools.partial(_moba_attn_kernel, n_blocks=n_blocks, n_sel=n_sel, tq=ATT_BLOCK),
                      "moba_attn", bsz, t, q, mkb, mvt, [kmean_pad, btile, bfar],
                      [pl.BlockSpec((1, nbp, LANES), lambda b, s, qt, kt: (b, 0, 0)),
                       _const_spec(btile), _const_spec(bfar)],
                      [pltpu.VMEM((nbp, ATT_ROWS * ATT_BLOCK), F32)])


def _round_bf16(x):
    return x.astype(BF16).astype(F32)


def _sample_attn_kernel(pt_ref, qm_ref, qd_ref, qo_ref, nkm_ref, ndk_ref, ndv_ref, nmk_ref, nmv_ref,
                        bias_d_ref, bias_o_ref, bnew_d_ref, bnew_o_ref,
                        ckv_hbm, krt_hbm, dk_hbm, dv_hbm, mkt_hbm, mvt_hbm,
                        omla_ref, odiff_ref, omoba_ref,
                        ckv_buf, krt_buf, dk_buf, dv_buf, mkt_buf, mvt_buf, sems,
                        *, layer, n_samples, n_chunks, n_sel):
    chunk = ckv_buf.shape[1]
    page = chunk // PAGES_PER_CHUNK
    blocks_per_chunk = chunk // MOBA_BLOCK
    n_blocks = n_chunks * blocks_per_chunk
    pools = (ckv_hbm, krt_hbm, dk_hbm, dv_hbm, mkt_hbm, mvt_hbm)
    bufs = (ckv_buf, krt_buf, dk_buf, dv_buf, mkt_buf, mvt_buf)
    key_major = (True, False, True, True, False, False)

    def page_copy(which, b, c, p, slot):
        pid = pt_ref[b, c * PAGES_PER_CHUNK + p]
        off = pl.ds(pl.multiple_of(p * page, page), page)
        dst = bufs[which].at[slot, off] if key_major[which] else bufs[which].at[slot, :, off]
        return pltpu.make_async_copy(pools[which].at[layer, pid], dst, sems.at[slot, which])

    def start_chunk(b, c, slot):
        def body(p, carry):
            for which in range(len(pools)):
                page_copy(which, b, c, p, slot).start()
            return carry
        lax.fori_loop(0, PAGES_PER_CHUNK, body, 0)

    def wait_chunk(b, c, slot):
        def body(p, carry):
            for which in range(len(pools)):
                page_copy(which, b, c, p, slot).wait()
            return carry
        lax.fori_loop(0, PAGES_PER_CHUNK, body, 0)

    def online(s, state):
        m, l, acc = state
        m_new = jnp.maximum(m, jnp.max(s, axis=1, keepdims=True))
        p = jnp.exp(s - m_new)
        alpha = jnp.exp(m - m_new)
        return (m_new, alpha * l + jnp.sum(p, axis=1, keepdims=True), alpha * acc), p.astype(BF16)

    def finish(state, s_new, v_new):
        m, l, acc = state
        m_new = jnp.maximum(m, s_new)
        p_new = jnp.exp(s_new - m_new)
        alpha = jnp.exp(m - m_new)
        return (alpha * acc + _round_bf16(p_new) * _round_bf16(v_new)) / (alpha * l + p_new)

    def new_score(q, k_new):
        return jnp.sum(_round_bf16(q) * _round_bf16(k_new), axis=1, keepdims=True)

    start_chunk(0, 0, 0)

    def sample_body(b, carry):
        qm = qm_ref[b].astype(BF16)
        qd = qd_ref[b].astype(BF16)
        qo = qo_ref[b].astype(BF16)
        fresh = (jnp.full((ATT_ROWS, 1), NEG_INF, F32), jnp.zeros((ATT_ROWS, 1), F32),
                 jnp.zeros((ATT_ROWS, LANES), F32))
        st_m, st_d = fresh, fresh
        blk_m, blk_l, blk_a = [], [], []
        kmean_t = jnp.zeros((LANES, LANES), F32)
        for c in range(n_chunks):
            slot = c % 2
            if c + 1 < n_chunks:
                start_chunk(b, c + 1, 1 - slot)
            else:
                @pl.when(b + 1 < n_samples)
                def _():
                    start_chunk(b + 1, 0, 1 - slot)
            wait_chunk(b, c, slot)
            ckv = ckv_buf[slot].astype(BF16)
            mkt = mkt_buf[slot]
            s_m = (_dot_nt(qm[:, :LANES], ckv)
                   + _dot(qm[:, LANES:LANES + MLA_ROPE], krt_buf[slot].astype(BF16))) * MLA_SCALE
            s_d = _dot_nt(qd, dk_buf[slot].astype(BF16)) + bias_d_ref[:, c * chunk:(c + 1) * chunk]
            s_o = _dot(qo, mkt.astype(BF16)) + bias_o_ref[:, c * chunk:(c + 1) * chunk]
            st_m, p_m = online(s_m, st_m)
            st_d, p_d = online(s_d, st_d)
            p_o = []
            for j in range(blocks_per_chunk):
                lo = j * MOBA_BLOCK
                s = s_o[:, lo:lo + MOBA_BLOCK]
                m_b = jnp.max(s, axis=1, keepdims=True)
                p = jnp.exp(s - m_b)
                blk_m.append(m_b)
                blk_l.append(jnp.sum(p, axis=1, keepdims=True))
                p_o.append(p.astype(BF16))
                mean = jnp.sum(mkt[:, lo:lo + MOBA_BLOCK], axis=1, keepdims=True) * (1.0 / MOBA_BLOCK)
                kmean_t = jnp.where(_lane_iota(kmean_t.shape) == c * blocks_per_chunk + j, mean, kmean_t)
            st_m = (st_m[0], st_m[1], st_m[2] + _dot(p_m, ckv))
            st_d = (st_d[0], st_d[1], st_d[2] + _dot(p_d, dv_buf[slot].astype(BF16)))
            for j in range(blocks_per_chunk):
                lo = j * MOBA_BLOCK
                blk_a.append(_dot_nt(p_o[j], mvt_buf[slot, :, lo:lo + MOBA_BLOCK].astype(BF16)))
        nkm = nkm_ref[b]
        omla_ref[b] = finish(st_m, new_score(qm, nkm) * MLA_SCALE, nkm[:, :LANES])
        odiff_ref[b] = finish(st_d, new_score(qd, ndk_ref[b]) + bnew_d_ref[:, :1], ndv_ref[b])
        sel = _topk_lanes(_dot(qo, kmean_t.astype(BF16)), n_blocks, n_sel)
        s_own = new_score(qo, nmk_ref[b]) + bnew_o_ref[:, :1]
        m_all = s_own
        for j in range(n_blocks):
            m_all = jnp.maximum(m_all, jnp.where(sel[:, j:j + 1] > 0.0, blk_m[j], NEG_INF))
        p_own = jnp.exp(s_own - m_all)
        l_all = p_own
        acc = _round_bf16(p_own) * _round_bf16(nmv_ref[b])
        for j in range(n_blocks):
            w = jnp.where(sel[:, j:j + 1] > 0.0, jnp.exp(blk_m[j] - m_all), 0.0)
            l_all = l_all + w * blk_l[j]
            acc = acc + w * blk_a[j]
        omoba_ref[b] = acc / l_all
        return carry

    lax.fori_loop(0, n_samples, sample_body, 0)


def _sample_attn_call(page_table, qm, qd, qo, nkm, ndk, ndv, nmk, nmv, bias_d, bias_o, bnew_d, bnew_o,
                      caches, layer):
    n_samples, n_pages = page_table.shape
    page = caches[0].shape[2]
    assert n_pages % PAGES_PER_CHUNK == 0 and (n_pages // PAGES_PER_CHUNK) % 2 == 0
    n_chunks = n_pages // PAGES_PER_CHUNK
    chunk = PAGES_PER_CHUNK * page
    assert chunk % MOBA_BLOCK == 0
    n_blocks = n_pages * page // MOBA_BLOCK
    assert n_blocks <= LANES
    n_sel = min(MOBA_TOPK, n_blocks)
    vmem = pl.BlockSpec(memory_space=pltpu.VMEM)
    out = jax.ShapeDtypeStruct((n_samples, ATT_ROWS, LANES), F32)
    buf_shapes = [(2, chunk, LANES), (2, MLA_ROPE, chunk), (2, chunk, LANES), (2, chunk, LANES),
                  (2, LANES, chunk), (2, LANES, chunk)]
    return pl.pallas_call(
        functools.partial(_sample_attn_kernel, layer=layer, n_samples=n_samples, n_chunks=n_chunks, n_sel=n_sel),
        in_specs=[pl.BlockSpec(memory_space=pltpu.SMEM)] + [vmem] * 12 + [pl.BlockSpec(memory_space=pl.ANY)] * 6,
        out_specs=[vmem] * 3,
        out_shape=[out, out, out],
        scratch_shapes=[pltpu.VMEM(s, F32) for s in buf_shapes] + [pltpu.SemaphoreType.DMA((2, len(caches)))],
        compiler_params=pltpu.CompilerParams(vmem_limit_bytes=VMEM_LIMIT),
        name="sample_attn",
    )(page_table, qm, qd, qo, nkm, ndk, ndv, nmk, nmv, bias_d, bias_o, bnew_d, bnew_o, *caches)


def _sample_post_kernel(omla_ref, odiff_ref, omoba_ref, wuv_ref, lq1_ref, lk1_ref, lq2_ref, lk2_ref, subln_ref,
                        mla_o, diff_o, moba_o, *, lam_init):
    _mla_out_proj([omla_ref[:, hd * LANES:(hd + 1) * LANES] for hd in range(MLA_HEADS)], wuv_ref, mla_o)
    lam = _diff_lambda(lq1_ref, lk1_ref, lq2_ref, lk2_ref, lam_init)
    for hd in range(DIFF_HEADS):
        n1 = odiff_ref[:, hd * LANES:(hd + 1) * LANES]
        n2 = odiff_ref[:, (DIFF_HEADS + hd) * LANES:(DIFF_HEADS + hd + 1) * LANES]
        diff_o[:, hd * LANES:(hd + 1) * LANES] = _diff_finish(
            n1, n2, lam, subln_ref[...], lam_init).astype(diff_o.dtype)
    rep = MOBA_HEADS // MOBA_KV_HEADS
    for j in range(rep):
        a = omoba_ref[:, j * LANES:(j + 1) * LANES]
        b = omoba_ref[:, (j + rep) * LANES:(j + rep + 1) * LANES]
        low = _lane_iota(a.shape) < MOBA_HD
        moba_o[:, j * LANES:(j + 1) * LANES] = jnp.where(low, a, b).astype(moba_o.dtype)


def _sample_post_call(omla, odiff, omoba, wuv, lams, subln, lam_init):
    n = omla.shape[0]
    out = jax.ShapeDtypeStruct((n, 4 * LANES), BF16)
    return pl.pallas_call(
        functools.partial(_sample_post_kernel, lam_init=lam_init),
        out_shape=[out, out, out],
        compiler_params=pltpu.CompilerParams(vmem_limit_bytes=VMEM_LIMIT),
        name="sample_post",
    )(omla.reshape(n, -1), odiff.reshape(n, -1), omoba.reshape(n, -1), wuv, *lams, subln)


def _merge_kernel(x_ref, g_ref, omla_ref, odiff_ref, omoba_ref, wg0_ref, wg1_ref, wg2_ref,
                  bg0_ref, bg1_ref, bg2_ref, wb0_ref, wb1_ref, wb2_ref, o_ref, h_scr):
    @pl.when(pl.program_id(1) == 0)
    def _():
        h_scr[...] = _rms(x_ref[...], g_ref[...]).astype(BF16)

    h = h_scr[...]
    merged = None
    for o_in, wg, bg, wb in ((omla_ref, wg0_ref, bg0_ref, wb0_ref), (odiff_ref, wg1_ref, bg1_ref, wb1_ref),
                             (omoba_ref, wg2_ref, bg2_ref, wb2_ref)):
        gate = jax.nn.sigmoid(_dot(h, wg[...]) + bg[...])
        term = gate * _dot(o_in[...], wb[...])
        merged = term if merged is None else merged + term
    o_ref[...] = merged.astype(o_ref.dtype)


def _merge_call(x, g, omla, odiff, omoba, wgate, bgate, wb_mla, wb_diff, wb_moba):
    n, d = x.shape
    tm = _tile(n, 512)
    tn = _tile(d, 512)
    nd = d // tn
    row = lambda w: pl.BlockSpec((tm, w), lambda i, j: (i, 0))
    gate_w = [pl.BlockSpec((d, tn), functools.partial(lambda i, j, br: (0, br * nd + j), br=br)) for br in range(3)]
    gate_b = [pl.BlockSpec((1, tn), functools.partial(lambda i, j, br: (0, br * nd + j), br=br)) for br in range(3)]
    branch_w = [pl.BlockSpec((w.shape[0], tn), lambda i, j: (0, j)) for w in (wb_mla, wb_diff, wb_moba)]
    return pl.pallas_call(
        _merge_kernel,
        grid=(n // tm, nd),
        in_specs=[row(d), pl.BlockSpec(g.shape, lambda i, j: (0, 0)),
                  row(omla.shape[1]), row(odiff.shape[1]), row(omoba.shape[1])] + gate_w + gate_b + branch_w,
        out_specs=pl.BlockSpec((tm, tn), lambda i, j: (i, j)),
        out_shape=jax.ShapeDtypeStruct((n, d), BF16),
        scratch_shapes=[pltpu.VMEM((tm, d), BF16)],
        compiler_params=_params(("parallel", "arbitrary")),
        name="gated_merge",
    )(x, g, omla, odiff, omoba, wgate, wgate, wgate, bgate, bgate, bgate, wb_mla, wb_diff, wb_moba)


def _out_proj_kernel(x_ref, merged_ref, wout_ref, gpost_ref, gmlp_ref, x1_ref, hn_ref):
    x1 = x_ref[...] + _rms(_dot(merged_ref[...], wout_ref[...]), gpost_ref[...])
    x1_ref[...] = x1
    hn_ref[...] = _rms(x1, gmlp_ref[...]).astype(hn_ref.dtype)


def _out_proj_call(x, merged, wout, gpost, gmlp):
    n, d = x.shape
    tm = _tile(n, 256)
    row = pl.BlockSpec((tm, d), lambda i: (i, 0))
    full = lambda a: pl.BlockSpec(a.shape, lambda i: (0, 0))
    return pl.pallas_call(
        _out_proj_kernel,
        grid=(n // tm,),
        in_specs=[row, row, full(wout), full(gpost), full(gmlp)],
        out_specs=[row, row],
        out_shape=[jax.ShapeDtypeStruct((n, d), F32), jax.ShapeDtypeStruct((n, d), BF16)],
        compiler_params=_params(("parallel",)),
        name="out_proj",
    )(x, merged, wout, gpost, gmlp)


def _mlp_kernel(hn_ref, wup_ref, wdown_ref, x1_ref, gpost_ref, x2_ref, acc_scr):
    k = pl.program_id(1)

    @pl.when(k == 0)
    def _():
        acc_scr[...] = jnp.zeros(acc_scr.shape, F32)

    u = jnp.square(jnp.maximum(_dot(hn_ref[...], wup_ref[...]), 0.0)).astype(BF16)
    acc_scr[...] += _dot(u, wdown_ref[...])

    @pl.when(k == pl.num_programs(1) - 1)
    def _():
        x2_ref[...] = x1_ref[...] + _rms(acc_scr[...], gpost_ref[...])


def _mlp_call(hn, wup, wdown, x1, gpost):
    n, d = x1.shape
    dff = wup.shape[1]
    tm = _tile(n, 512)
    tf = _tile(dff, 1024)
    row = pl.BlockSpec((tm, d), lambda i, k: (i, 0))
    return pl.pallas_call(
        _mlp_kernel,
        grid=(n // tm, dff // tf),
        in_specs=[row, pl.BlockSpec((d, tf), lambda i, k: (0, k)), pl.BlockSpec((tf, d), lambda i, k: (k, 0)),
                  row, pl.BlockSpec(gpost.shape, lambda i, k: (0, 0))],
        out_specs=row,
        out_shape=jax.ShapeDtypeStruct((n, d), F32),
        scratch_shapes=[pltpu.VMEM((tm, d), F32)],
        compiler_params=_params(("parallel", "arbitrary")),
        name="relu2_mlp",
    )(hn, wup, wdown, x1, gpost)


def _ple_kernel(x2_ref, p_ref, gple_ref, wpg_ref, wple_ref, x3_ref):
    x2 = x2_ref[...]
    gate = jax.nn.sigmoid(_dot(_rms(x2, gple_ref[...]).astype(BF16), wpg_ref[...]))
    x3_ref[...] = x2 + gate * _dot(p_ref[...].astype(BF16), wple_ref[...])


def _ple_call(x2, p, gple, wpg, wple):
    n, d = x2.shape
    tm = _tile(n, 256)
    row = lambda w: pl.BlockSpec((tm, w), lambda i: (i, 0))
    full = lambda a: pl.BlockSpec(a.shape, lambda i: (0, 0))
    return pl.pallas_call(
        _ple_kernel,
        grid=(n // tm,),
        in_specs=[row(d), row(p.shape[1]), full(gple), full(wpg), full(wple)],
        out_specs=row(d),
        out_shape=jax.ShapeDtypeStruct((n, d), F32),
        compiler_params=_params(("parallel",)),
        name="ple_gate",
    )(x2, p, gple, wpg, wple)


def _pad_cols(w, width):
    return jnp.pad(w, ((0, 0), (0, width - w.shape[1])))


def _rot_cols(w):
    half = w.shape[1] // 2
    return jnp.concatenate([-w[:, half:], w[:, :half]], axis=1)


def _prep_layer(w_in, w_uq, w_ukv, wb_moba):
    sizes = (MLA_Q_LORA, MLA_KV_LORA, MLA_ROPE, DIFF_HEADS * 2 * DIFF_HD, 2 * DIFF_HD, 2 * DIFF_HD,
             MOBA_HEADS * MOBA_HD, MOBA_KV_HEADS * MOBA_HD, MOBA_KV_HEADS * MOBA_HD)
    offs = [0]
    for s in sizes:
        offs.append(offs[-1] + s)
    cq, ckv, kr, dq, dk, dv, mq, mk, mv = [w_in[:, offs[i]:offs[i + 1]] for i in range(len(sizes))]
    win = jnp.concatenate([cq, ckv, _pad_cols(kr, LANES), _pad_cols(_rot_cols(kr), LANES),
                           dq, dk, dv, mq, mk, mv], axis=1).astype(BF16)
    assert win.shape[1] == _C_END
    per_head = MLA_NOPE + MLA_ROPE
    nope, rope, rot = [], [], []
    for hd in range(MLA_HEADS):
        blk = w_uq[:, hd * per_head:(hd + 1) * per_head]
        nope.append(_pad_cols(blk[:, :MLA_NOPE], LANES))
        rope.append(_pad_cols(blk[:, MLA_NOPE:], LANES))
        rot.append(_pad_cols(_rot_cols(blk[:, MLA_NOPE:]), LANES))
    wq = jnp.concatenate(nope + rope + rot, axis=1).astype(BF16)
    w3 = w_ukv.reshape(MLA_KV_LORA, MLA_HEADS, MLA_NOPE + MLA_V)
    wuk = jnp.pad(jnp.transpose(w3[..., :MLA_NOPE], (1, 2, 0)), ((0, 0), (0, LANES - MLA_NOPE), (0, 0))).astype(BF16)
    wuv_h = jnp.transpose(w3[..., MLA_NOPE:], (1, 0, 2))
    zero = jnp.zeros_like(wuv_h[0])
    wuv = jnp.stack([jnp.concatenate([jnp.concatenate([wuv_h[2 * j], zero], axis=1),
                                      jnp.concatenate([zero, wuv_h[2 * j + 1]], axis=1)], axis=0)
                     for j in range(MLA_HEADS // 2)]).astype(BF16)
    rep = MOBA_HEADS // MOBA_KV_HEADS
    order = [hd for j in range(rep) for hd in (j, j + rep)]
    wbm = jnp.concatenate([wb_moba[hd * MOBA_HD:(hd + 1) * MOBA_HD] for hd in order], axis=0).astype(BF16)
    return win, wq, wuk, wuv, wbm


def _rope_tables(pos):
    half = MLA_ROPE // 2
    inv = ROPE_THETA ** (-jnp.arange(half, dtype=F32) / half)
    ang = pos.astype(F32)[:, None] * inv[None, :]
    cos = jnp.cos(ang)
    sin = jnp.sin(ang)
    return (_pad_cols(jnp.concatenate([cos, cos], axis=1), LANES),
            _pad_cols(jnp.concatenate([sin, sin], axis=1), LANES))


def _tile_bias(per_head):
    h, two, tk, tq = per_head.shape
    return jnp.transpose(per_head, (1, 2, 0, 3)).reshape(two, tk, h * tq)


def kernel(x_prompt, x_sample, cache_mla_ckv, cache_mla_krope, cache_diff_k, cache_diff_v, cache_moba_k,
           cache_moba_v, page_table, p_prompt, p_sample, rel_bias_table, norm_pre_mix, norm_post_mix,
           norm_pre_mlp, norm_post_mlp, w_in, mla_q_norm, mla_kv_norm, mla_w_uq, mla_w_ukv, diff_lambda_q1,
           diff_lambda_k1, diff_lambda_q2, diff_lambda_k2, diff_subln, w_branch_mla, w_branch_diff,
           w_branch_moba, w_gate, b_gate, w_out, w_up, w_down, ple_norm, w_ple_gate, w_ple):
    depth = w_in.shape[0]
    bsz, t, d = x_prompt.shape
    n_s = x_sample.shape[0]
    tb = ATT_BLOCK
    assert x_sample.shape[1] == 1 and t % tb == 0 and n_s <= tb
    n_pool, page = cache_mla_ckv.shape[1], cache_mla_ckv.shape[2]
    past = page_table.shape[1] * page
    caches = (cache_mla_ckv,
              jnp.transpose(cache_mla_krope, (0, 1, 3, 2)),
              cache_diff_k.reshape(depth, n_pool, page, LANES), cache_diff_v.reshape(depth, n_pool, page, LANES),
              jnp.transpose(cache_moba_k, (0, 1, 3, 4, 2)).reshape(depth, n_pool, LANES, page),
              jnp.transpose(cache_moba_v, (0, 1, 3, 4, 2)).reshape(depth, n_pool, LANES, page))

    cos_p, sin_p = _rope_tables(jnp.arange(t, dtype=jnp.int32))
    cos_p, sin_p = jnp.tile(cos_p, (bsz, 1)), jnp.tile(sin_p, (bsz, 1))
    cos_s, sin_s = _rope_tables(jnp.full((n_s,), past, dtype=jnp.int32))
    ii = jnp.arange(tb, dtype=jnp.int32)
    delta = ii[None, :] - ii[:, None]
    dist = jnp.concatenate([jnp.maximum(delta, 0), delta + tb], axis=0)
    btile = _bias_gather(rel_bias_table, _t5_bucket(dist)).reshape(-1, 2, tb, tb)
    far_bucket = N_BUCKETS - 1
    assert tb + 1 >= MAX_DISTANCE
    bfar = jnp.repeat(rel_bias_table[far_bucket], tb)[None, :]
    n_d = DIFF_HEADS * tb
    btile_d = _tile_bias(jnp.concatenate([btile[:DIFF_HEADS], btile[:DIFF_HEADS]], axis=0))
    btile_o = _tile_bias(btile[DIFF_HEADS:])
    bfar_d = jnp.concatenate([bfar[:, :n_d], bfar[:, :n_d]], axis=1)
    bfar_o = bfar[:, n_d:]
    dist_s = (past - jnp.arange(past, dtype=jnp.int32))[None, :]
    bias_keys = _bias_gather(rel_bias_table, _t5_bucket(dist_s))[:, 0, :]
    bias_d = jnp.concatenate([bias_keys[:DIFF_HEADS], bias_keys[:DIFF_HEADS]], axis=0)
    bias_o = bias_keys[DIFF_HEADS:]
    bnew = jnp.broadcast_to(rel_bias_table[0][:, None], (DIFF_HEADS + MOBA_HEADS, LANES))
    bnew_d = jnp.concatenate([bnew[:DIFF_HEADS], bnew[:DIFF_HEADS]], axis=0)
    bnew_o = bnew[DIFF_HEADS:]

    xp = x_prompt.reshape(bsz * t, d)
    xs = x_sample.reshape(n_s, d)
    rows_p, rows_s = [], []
    r2 = lambda a: a.reshape(1, -1)
    nblk = t // MOBA_BLOCK
    nbp = -(-nblk // SUBLANES) * SUBLANES
    for i in range(depth):
        lam_init = 0.8 - 0.6 * math.exp(-0.3 * i)
        win, wq, wuk, wuv, wbm = _prep_layer(w_in[i], mla_w_uq[i], mla_w_ukv[i], w_branch_moba[i])
        lams = [r2(diff_lambda_q1[i]), r2(diff_lambda_k1[i]), r2(diff_lambda_q2[i]), r2(diff_lambda_k2[i])]
        subln = r2(diff_subln[i])
        wgate, bgate = w_gate[i].astype(BF16), r2(b_gate[i])
        wb_mla, wb_diff = w_branch_mla[i].astype(BF16), w_branch_diff[i].astype(BF16)
        wout, wup, wdown = w_out[i].astype(BF16), w_up[i].astype(BF16), w_down[i].astype(BF16)
        wpg, wple = w_ple_gate[i].astype(BF16), w_ple[i].astype(BF16)
        pre_w = (r2(norm_pre_mix[i]), win, r2(mla_q_norm[i]), r2(mla_kv_norm[i]), wq, wuk)

        def tail(x, omla, odiff, omoba, p):
            merged = _merge_call(x, r2(norm_pre_mix[i]), omla, odiff, omoba, wgate, bgate, wb_mla, wb_diff, wbm)
            x1, hn = _out_proj_call(x, merged, wout, r2(norm_post_mix[i]), r2(norm_pre_mlp[i]))
            x2 = _mlp_call(hn, wup, wdown, x1, r2(norm_post_mlp[i]))
            return _ple_call(x2, p, r2(ple_norm[i]), wpg, wple)

        (ckv, kr, dk, dv, mk, mv, kcat, ckvt, dkb, dvt, mkb, mvt, qm, qd, qo, kmean) = _pre_call(
            xp, *pre_w, cos_p, sin_p, with_kmean=True)
        rows_p.append((ckv, kr, dk, dv, mk, mv))
        kmean_pad = jnp.pad(kmean[:, 0, :].reshape(bsz, nblk, LANES), ((0, 0), (0, nbp - nblk), (0, 0)))
        flat = lambda q: q.reshape(-1, q.shape[-1])
        omla = _mla_attn_call(flat(qm), kcat, ckvt, wuv, bsz, t)
        odiff = _diff_attn_call(flat(qd), dkb, dvt, btile_d, bfar_d, lams, subln, bsz, t, lam_init)
        omoba = _moba_attn_call(flat(qo), mkb, mvt, kmean_pad, btile_o, bfar_o, bsz, t)
        xp = tail(xp, omla, odiff, omoba, p_prompt[i].reshape(bsz * t, -1))

        (ckv, kr, dk, dv, mk, mv, _, _, _, _, _, _, qm, qd, qo, _) = _pre_call(
            xs, *pre_w, cos_s, sin_s, with_kmean=False)
        rows_s.append((ckv, kr, dk, dv, mk, mv))
        per_sample = lambda q: jnp.transpose(q[0], (1, 0, 2)).astype(F32)
        nkm = jnp.concatenate([ckv, _pad_cols(kr, LANES)], axis=1)[:, None, :]
        a_mla, a_diff, a_moba = _sample_attn_call(
            page_table, per_sample(qm), per_sample(qd), per_sample(qo), nkm,
            dk[:, None, :], dv[:, None, :], mk[:, None, :], mv[:, None, :],
            bias_d, bias_o, bnew_d, bnew_o, caches, i)
        omla, odiff, omoba = _sample_post_call(a_mla, a_diff, a_moba, wuv, lams, subln, lam_init)
        xs = tail(xs, omla, odiff, omoba, p_sample[i].reshape(n_s, -1))

    def stack(rows, lead):
        cols = list(zip(*rows))
        shapes = [(MLA_KV_LORA,), (MLA_ROPE,), (1, 2 * DIFF_HD), (1, 2 * DIFF_HD),
                  (MOBA_KV_HEADS, MOBA_HD), (MOBA_KV_HEADS, MOBA_HD)]
        return tuple(jnp.stack(c, axis=0).reshape((depth,) + lead + s) for c, s in zip(cols, shapes))

    return ((xp.reshape(bsz, t, d), xs.reshape(n_s, 1, d)) + stack(rows_p, (bsz, t)) + stack(rows_s, (n_s, 1)))
```
